```python
import math
import jax, jax.numpy as jnp
from jax import lax
import numpy as np

D_MODEL = 2048
BATCH = 8
SEQ = 2048
DEPTH = 2

A_HEADS = 8
A_HEAD_DIM = 64
A_WIDTH = A_HEADS * 2 * A_HEAD_DIM
B_HEADS = 8
B_HEAD_DIM = 128
B_WIDTH = B_HEADS * B_HEAD_DIM
B_GROUPS = ((128, 1), (512, 4), (2048, 16))
C_WIDTH = 1024
C_CONV = 3
N_BRANCH = 3
Q_BLOCK = 128
N_ALIBI = A_HEADS + len(B_GROUPS) * B_HEADS
RMS_EPS = 1e-6

SPLIT_SIZES = ([A_WIDTH] * 4
               + [B_WIDTH] * (3 * len(B_GROUPS))
               + [B_WIDTH]
               + [C_WIDTH] * 4
               + [N_BRANCH * D_MODEL])
N_IN = int(sum(SPLIT_SIZES))
SPLIT_POINTS = [int(v) for v in np.cumsum(SPLIT_SIZES)[:-1]]

kernel_name = "hybrid_diffattn_dilated_shortconv_gated_merge"


def _rmsnorm(x, g):
    xf = x.astype(jnp.float32)
    y = xf * lax.rsqrt(jnp.mean(xf * xf, axis=-1, keepdims=True) + RMS_EPS)
    return (y * g.astype(jnp.float32)).astype(x.dtype)


def _alibi_slopes(n):
    return jnp.asarray(2.0 ** (-8.0 * np.arange(1, n + 1) / n), dtype=jnp.float32)


def _diff_attention(q, k, v, lam, slopes):
    bsz, seq = q.shape[:2]
    nb = seq // Q_BLOCK
    q = q.reshape(bsz, nb, Q_BLOCK, A_HEADS, 2, A_HEAD_DIM).transpose(1, 0, 2, 3, 4, 5)
    k = k.reshape(bsz, seq, A_HEADS, 2, A_HEAD_DIM)
    v = v.reshape(bsz, seq, A_HEADS, 2 * A_HEAD_DIM)
    kpos = jnp.arange(seq)
    scale = A_HEAD_DIM ** -0.5

    def one_block(args):
        qb, i = args
        s = jnp.einsum('bqhmd,bkhmd->bhmqk', qb, k, preferred_element_type=jnp.float32) * scale
        dist = (i * Q_BLOCK + jnp.arange(Q_BLOCK))[:, None] - kpos[None, :]
        s = s - slopes[:, None, None, None] * dist.astype(jnp.float32)
        s = jnp.where(dist >= 0, s, -jnp.inf)
        p = jax.nn.softmax(s, axis=-1)
        p = p[:, :, 0] - lam * p[:, :, 1]
        return jnp.einsum('bhqk,bkhe->bqhe', p.astype(v.dtype), v)

    o = lax.map(one_block, (q, jnp.arange(nb)))
    return o.transpose(1, 0, 2, 3, 4).reshape(bsz, seq, A_HEADS, 2 * A_HEAD_DIM)


def _dilated_group(q, k, v, dilation, n_offsets, slopes):
    bsz, seq, nh, hd = q.shape
    sub = seq // dilation
    nb = -(-sub // Q_BLOCK)
    subp = nb * Q_BLOCK

    def to_sub(t):
        t = t.reshape(bsz, sub, dilation, nh, hd).transpose(0, 2, 1, 3, 4)
        return jnp.pad(t, ((0, 0), (0, 0), (0, subp - sub), (0, 0), (0, 0)))

    def band(t):
        t = jnp.pad(to_sub(t), ((0, 0), (0, 0), (Q_BLOCK, 0), (0, 0), (0, 0)))
        t = t.reshape(bsz, dilation, nb + 1, Q_BLOCK, nh, hd)
        return jnp.concatenate([t[:, :, :-1], t[:, :, 1:]], axis=3)

    qs = to_sub(q).reshape(bsz, dilation, nb, Q_BLOCK, nh, hd)
    kb, vb = band(k), band(v)
    s = jnp.einsum('brnqhd,brnkhd->brnhqk', qs, kb,
                   preferred_element_type=jnp.float32) * (hd ** -0.5)
    qi = jnp.arange(Q_BLOCK)[:, None]
    ki = jnp.arange(2 * Q_BLOCK)[None, :]
    off = Q_BLOCK + qi - ki
    key_idx = (jnp.arange(nb)[:, None, None] - 1) * Q_BLOCK + ki
    valid = (off >= 0) & (off <= n_offsets) & (key_idx >= 0)
    s = s - slopes[:, None, None] * (off * dilation).astype(jnp.float32)
    s = jnp.where(valid[:, None], s, -jnp.inf)
    lse = jax.nn.logsumexp(s, axis=-1, keepdims=True)
    p = jnp.exp(s - lse)
    o = jnp.einsum('brnhqk,brnkhd->brnqhd', p.astype(v.dtype), vb)
    o = o.reshape(bsz, dilation, subp, nh, hd)[:, :, :sub]
    o = o.transpose(0, 2, 1, 3, 4).reshape(bsz, seq, nh, hd)
    lse = lse[..., 0].transpose(0, 1, 2, 4, 3).reshape(bsz, dilation, subp, nh)[:, :, :sub]
    lse = lse.transpose(0, 2, 1, 3).reshape(bsz, seq, nh)
    return o, lse


def setup_inputs(seed: int = 0) -> dict:
    key = jax.random.key(seed)
    ks = jax.random.split(key, 16)
    f32 = jnp.float32
    nrm = lambda k, shape, s: (jax.random.normal(k, shape, f32) * s).astype(f32)
    return {
        "x": nrm(ks[0], (BATCH, SEQ, D_MODEL), 1.0),
        "norm_g": 1.0 + nrm(ks[1], (DEPTH, D_MODEL), 0.02),
        "w_in": nrm(ks[2], (DEPTH, D_MODEL, N_IN), D_MODEL ** -0.5),
        "gate_b": nrm(ks[3], (DEPTH, N_BRANCH * D_MODEL), 0.02),
        "lambda_q1": nrm(ks[4], (DEPTH, A_HEAD_DIM), 0.1),
        "lambda_k1": nrm(ks[5], (DEPTH, A_HEAD_DIM), 0.1),
        "lambda_q2": nrm(ks[6], (DEPTH, A_HEAD_DIM), 0.1),
        "lambda_k2": nrm(ks[7], (DEPTH, A_HEAD_DIM), 0.1),
        "subln_g": 1.0 + nrm(ks[8], (DEPTH, 2 * A_HEAD_DIM), 0.02),
        "conv_w": nrm(ks[9], (DEPTH, C_CONV, C_WIDTH), C_CONV ** -0.5),
        "w_branch_a": nrm(ks[10], (DEPTH, A_WIDTH, D_MODEL), A_WIDTH ** -0.5),
        "w_branch_b": nrm(ks[11], (DEPTH, B_WIDTH, D_MODEL), B_WIDTH ** -0.5),
        "w_branch_c": nrm(ks[12], (DEPTH, C_WIDTH, D_MODEL), C_WIDTH ** -0.5),
        "w_out": nrm(ks[13], (DEPTH, D_MODEL, D_MODEL), D_MODEL ** -0.5),
        "final_g": 1.0 + nrm(ks[14], (D_MODEL,), 0.02),
    }


def reference(x, norm_g, w_in, gate_b, lambda_q1, lambda_k1, lambda_q2, lambda_k2,
              subln_g, conv_w, w_branch_a, w_branch_b, w_branch_c, w_out, final_g):
    bsz, seq, _ = x.shape
    slopes = _alibi_slopes(N_ALIBI)
    slopes_a = slopes[:A_HEADS]
    n_g = len(B_GROUPS)
    for l in range(DEPTH):
        h = _rmsnorm(x, norm_g[l])
        proj = jnp.einsum('bsd,df->bsf', h, w_in[l])
        parts = jnp.split(proj, SPLIT_POINTS, axis=-1)
        aq, ak, av, az = parts[0:4]
        b_qkv = parts[4:4 + 3 * n_g]
        bz = parts[4 + 3 * n_g]
        cb, cc, cx, cz = parts[5 + 3 * n_g:9 + 3 * n_g]
        gates = jax.nn.sigmoid(parts[9 + 3 * n_g] + gate_b[l]).reshape(bsz, seq, N_BRANCH, D_MODEL)

        lam_init = 0.8 - 0.6 * math.exp(-0.3 * l)
        lam = (jnp.exp(jnp.sum(lambda_q1[l] * lambda_k1[l]).astype(jnp.float32))
               - jnp.exp(jnp.sum(lambda_q2[l] * lambda_k2[l]).astype(jnp.float32)) + lam_init)
        oa = _diff_attention(aq, ak, av, lam, slopes_a)
        oa = (_rmsnorm(oa, subln_g[l]) * (1.0 - lam_init)).reshape(bsz, seq, A_WIDTH)
        ya = oa * jax.nn.silu(az)

        outs, lses = [], []
        for g, (window, dil) in enumerate(B_GROUPS):
            q, k, v = [t.reshape(bsz, seq, B_HEADS, B_HEAD_DIM) for t in b_qkv[3 * g:3 * g + 3]]
            o, lse = _dilated_group(q, k, v, dil, window // dil,
                                    slopes[A_HEADS + g * B_HEADS:A_HEADS + (g + 1) * B_HEADS])
            outs.append(o)
            lses.append(lse)
        wgt = jax.nn.softmax(jnp.stack(lses, axis=0), axis=0)
        ob = jnp.sum(wgt[..., None].astype(x.dtype) * jnp.stack(outs, axis=0), axis=0)
        yb = ob.reshape(bsz, seq, B_WIDTH) * jax.nn.silu(bz)

        u = jnp.pad(cc * cx, ((0, 0), (C_CONV - 1, 0), (0, 0)))
        conv = sum(conv_w[l][j] * u[:, j:j + seq] for j in range(C_CONV))
        yc = cb * conv * jax.nn.silu(cz)

        merged = (gates[:, :, 0] * jnp.einsum('bsw,wd->bsd', ya, w_branch_a[l])
                  + gates[:, :, 1] * jnp.einsum('bsw,wd->bsd', yb, w_branch_b[l])
                  + gates[:, :, 2] * jnp.einsum('bsw,wd->bsd', yc, w_branch_c[l]))
        x = x + jnp.einsum('bsd,de->bse', merged, w_out[l])
    return _rmsnorm(x, final_g)
```

```python
import functools
import math

import jax
import jax.numpy as jnp
import numpy as np
from jax import lax
from jax.experimental import pallas as pl
from jax.experimental.pallas import tpu as pltpu

D_MODEL = 2048
SEQ = 2048
DEPTH = 2
A_HEADS = 8
A_HEAD_DIM = 64
B_HEADS = 8
B_HEAD_DIM = 128
B_GROUPS = ((128, 1), (512, 4), (2048, 16))
WIDTH = 1024
C_CONV = 3
N_BRANCH = 3
Q_BLOCK = 128
N_ALIBI = A_HEADS + len(B_GROUPS) * B_HEADS
RMS_EPS = 1e-6
N_IN = 4 * WIDTH + 9 * WIDTH + WIDTH + 4 * WIDTH + N_BRANCH * D_MODEL

LANE = 128
N_SLABS = N_IN // LANE
SLABS = WIDTH // LANE
SLAB_AQ, SLAB_AK, SLAB_AV, SLAB_AZ = 0, SLABS, 2 * SLABS, 3 * SLABS
SLAB_B = 4 * SLABS
SLAB_BZ = SLAB_B + 9 * SLABS
SLAB_CB = SLAB_BZ + SLABS
SLAB_CC, SLAB_CX, SLAB_CZ = SLAB_CB + SLABS, SLAB_CB + 2 * SLABS, SLAB_CB + 3 * SLABS
SLAB_GATE = SLAB_CB + 4 * SLABS
GATE_SLABS = N_BRANCH * D_MODEL // LANE

NEG_BIG = -1e30
VMEM_LIMIT = 56 * 1024 * 1024

_BF16 = jnp.bfloat16
_F32 = jnp.float32


def _silu(z):
    return z * jax.nn.sigmoid(z)


def _inproj_kernel(x_ref, g_ref, w_ref, o_ref, h_ref):
    @pl.when(pl.program_id(1) == 0)
    def _():
        x = x_ref[...]
        ms = jnp.mean(x * x, axis=-1, keepdims=True)
        h_ref[...] = (x * lax.rsqrt(ms + RMS_EPS) * g_ref[...]).astype(_BF16)

    acc = jnp.dot(h_ref[...], w_ref[...], preferred_element_type=_F32)
    for c in range(o_ref.shape[0]):
        o_ref[c] = acc[:, c * LANE:(c + 1) * LANE].astype(o_ref.dtype)


def _inproj(x2d, g, w, *, tm=1024, tn=1024):
    tokens = x2d.shape[0]
    return pl.pallas_call(
        _inproj_kernel,
        grid=(tokens // tm, N_IN // tn),
        in_specs=[
            pl.BlockSpec((tm, D_MODEL), lambda i, j: (i, 0)),
            pl.BlockSpec((1, D_MODEL), lambda i, j: (0, 0)),
            pl.BlockSpec((D_MODEL, tn), lambda i, j: (0, j)),
        ],
        out_specs=pl.BlockSpec((tn // LANE, tm, LANE), lambda i, j: (j, i, 0)),
        out_shape=jax.ShapeDtypeStruct((N_SLABS, tokens, LANE), _BF16),
        scratch_shapes=[pltpu.VMEM((tm, D_MODEL), _BF16)],
        compiler_params=pltpu.CompilerParams(
            dimension_semantics=("arbitrary", "arbitrary"), vmem_limit_bytes=VMEM_LIMIT),
        name="inproj",
    )(x2d, g.reshape(1, D_MODEL), w)


def _attn_a_kernel(slopes_ref, lamv_ref, q_ref, k_ref, v_ref, z_ref, sg_ref, o_ref, *,
                   lam_init, tq):
    i = pl.program_id(2)
    slope = slopes_ref[pl.program_id(1)]
    tk = tq

    lane = lax.broadcasted_iota(jnp.int32, (tq, LANE), 1)
    qf = q_ref[...].astype(_F32) * (A_HEAD_DIM ** -0.5)
    qq = jnp.concatenate([jnp.where(lane < A_HEAD_DIM, qf, 0.0),
                          jnp.where(lane >= A_HEAD_DIM, qf, 0.0)], axis=0).astype(_BF16)

    def step(j, carry, masked):
        m, l, acc = carry
        start = pl.multiple_of(j * tk, tk)
        k = k_ref[pl.ds(start, tk), :]
        v = v_ref[pl.ds(start, tk), :]
        s = lax.dot_general(qq, k, (((1,), (1,)), ((), ())), preferred_element_type=_F32)
        krel = (j * tk - i * tq) + lax.broadcasted_iota(jnp.int32, (1, tk), 1)
        s = s + slope * krel.astype(_F32)
        if masked:
            row = lax.broadcasted_iota(jnp.int32, (2 * tq, tk), 0)
            row = jnp.where(row >= tq, row - tq, row)
            col = lax.broadcasted_iota(jnp.int32, (2 * tq, tk), 1)
            s = jnp.where(col <= row, s, NEG_BIG)
        m_new = jnp.maximum(m, jnp.max(s, axis=-1, keepdims=True))
        alpha = jnp.exp(m - m_new)
        p = jnp.exp(s - m_new)
        l = alpha * l + jnp.sum(p, axis=-1, keepdims=True)
        acc = alpha * acc + jnp.dot(p.astype(_BF16), v, preferred_element_type=_F32)
        return m_new, l, acc

    init = (jnp.full((2 * tq, 1), NEG_BIG, _F32), jnp.zeros((2 * tq, 1), _F32),
            jnp.zeros((2 * tq, LANE), _F32))
    carry = lax.fori_loop(0, i, functools.partial(step, masked=False), init)
    _, l, acc = step(i, carry, masked=True)

    lv = lamv_ref[...]
    lam = (jnp.exp(jnp.sum(lv[0:1] * lv[1:2], axis=-1, keepdims=True))
           - jnp.exp(jnp.sum(lv[2:3] * lv[3:4], axis=-1, keepdims=True)) + lam_init)
    o = acc / l
    out = o[:tq] - lam * o[tq:]
    ms = jnp.mean(out * out, axis=-1, keepdims=True)
    oa = out * lax.rsqrt(ms + RMS_EPS) * sg_ref[...] * (1.0 - lam_init)
    o_ref[...] = (oa * _silu(z_ref[...].astype(_F32))).astype(o_ref.dtype)


def _attn_a(proj4, slopes, lamv, subln_g, *, lam_init, tq=512):
    bsz, seq = proj4.shape[1], proj4.shape[2]
    return pl.pallas_call(
        functools.partial(_attn_a_kernel, lam_init=lam_init, tq=tq),
        grid=(bsz, A_HEADS, seq // tq),
        in_specs=[
            pl.BlockSpec(memory_space=pltpu.SMEM),
            pl.BlockSpec((4, A_HEAD_DIM), lambda b, h, i: (0, 0)),
            pl.BlockSpec((None, None, tq, LANE), lambda b, h, i: (SLAB_AQ + h, b, i, 0)),
            pl.BlockSpec((None, None, seq, LANE), lambda b, h, i: (SLAB_AK + h, b, 0, 0)),
            pl.BlockSpec((None, None, seq, LANE), lambda b, h, i: (SLAB_AV + h, b, 0, 0)),
            pl.BlockSpec((None, None, tq, LANE), lambda b, h, i: (SLAB_AZ + h, b, i, 0)),
            pl.BlockSpec((1, LANE), lambda b, h, i: (0, 0)),
        ],
        out_specs=pl.BlockSpec((None, None, tq, LANE), lambda b, h, i: (h, b, i, 0)),
        out_shape=jax.ShapeDtypeStruct((SLABS, bsz, seq, LANE), _BF16),
        compiler_params=pltpu.CompilerParams(
            dimension_semantics=("arbitrary", "arbitrary", "arbitrary"),
            vmem_limit_bytes=VMEM_LIMIT),
        name="attn_a",
    )(slopes, lamv, proj4, proj4, proj4, proj4, subln_g.reshape(1, LANE))


def _band_bias(slope, dil):
    qi = lax.broadcasted_iota(jnp.int32, (Q_BLOCK, 2 * Q_BLOCK), 0)
    ki = lax.broadcasted_iota(jnp.int32, (Q_BLOCK, 2 * Q_BLOCK), 1)
    off = Q_BLOCK + qi - ki
    valid = (off >= 0) & (off <= Q_BLOCK)
    return jnp.where(valid, -(slope * dil) * off.astype(_F32), NEG_BIG)


def _band_block(q, k, v, bias):
    s = lax.dot_general(q, k, (((1,), (1,)), ((), ())), preferred_element_type=_F32)
    s = s * (B_HEAD_DIM ** -0.5) + bias
    m = jnp.max(s, axis=-1, keepdims=True)
    p = jnp.exp(s - m)
    l = jnp.sum(p, axis=-1, keepdims=True)
    o = jnp.dot(p.astype(_BF16), v, preferred_element_type=_F32) / l
    return o, m + jnp.log(l)


def _attn_b_kernel(slopes_ref, q1_ref, k1_ref, v1_ref, q2_ref, k2_ref, v2_ref,
                   q3_ref, k3_ref, v3_ref, z_ref, o_ref, f_ref, og_ref, lg_ref):
    seq = o_ref.shape[0]
    h = pl.program_id(1)
    group_refs = ((q1_ref, k1_ref, v1_ref), (q2_ref, k2_ref, v2_ref), (q3_ref, k3_ref, v3_ref))

    for g, (window, dil) in enumerate(B_GROUPS):
        assert window // dil == Q_BLOCK
        slope = slopes_ref[A_HEADS + g * B_HEADS + h]
        bias = _band_bias(slope, dil)
        bias0 = bias[:, Q_BLOCK:]
        qr, kr, vr = group_refs[g]
        sub = seq // dil
        if dil > 1:
            f_ref[0] = qr[...].astype(_F32)
            f_ref[1] = kr[...].astype(_F32)
            f_ref[2] = vr[...].astype(_F32)
        for r in range(dil):
            for n in range(sub // Q_BLOCK):
                first = n == 0
                k0 = n * Q_BLOCK if first else (n - 1) * Q_BLOCK
                nk = Q_BLOCK if first else 2 * Q_BLOCK
                if dil == 1:
                    q_rows = pl.ds(n * Q_BLOCK, Q_BLOCK)
                    q, k, v = qr[q_rows, :], kr[pl.ds(k0, nk), :], vr[pl.ds(k0, nk), :]
                else:
                    q_rows = pl.ds(n * Q_BLOCK * dil + r, Q_BLOCK, stride=dil)
                    k_rows = pl.ds(k0 * dil + r, nk, stride=dil)
                    q = f_ref[0, q_rows, :].astype(_BF16)
                    k = f_ref[1, k_rows, :].astype(_BF16)
                    v = f_ref[2, k_rows, :].astype(_BF16)
                o, lse = _band_block(q, k, v, bias0 if first else bias)
                og_ref[g, q_rows, :] = o
                lg_ref[g, q_rows, :] = jnp.broadcast_to(lse, (Q_BLOCK, LANE))

    rows = 256
    for c in range(seq // rows):
        sl = pl.ds(c * rows, rows)
        l0, l1, l2 = lg_ref[0, sl, :], lg_ref[1, sl, :], lg_ref[2, sl, :]
        mx = jnp.maximum(jnp.maximum(l0, l1), l2)
        w0, w1, w2 = jnp.exp(l0 - mx), jnp.exp(l1 - mx), jnp.exp(l2 - mx)
        ob = (w0 * og_ref[0, sl, :] + w1 * og_ref[1, sl, :] + w2 * og_ref[2, sl, :]) / (w0 + w1 + w2)
        o_ref[sl, :] = (ob * _silu(z_ref[sl, :].astype(_F32))).astype(o_ref.dtype)


def _attn_b(proj4, slopes):
    bsz, seq = proj4.shape[1], proj4.shape[2]

    def slab(first):
        return pl.BlockSpec((None, None, seq, LANE), lambda b, h: (first + h, b, 0, 0))

    in_specs = [pl.BlockSpec(memory_space=pltpu.SMEM)]
    for g in range(len(B_GROUPS)):
        for part in range(3):
            in_specs.append(slab(SLAB_B + (3 * g + part) * SLABS))
    in_specs.append(slab(SLAB_BZ))
    return pl.pallas_call(
        _attn_b_kernel,
        grid=(bsz, B_HEADS),
        in_specs=in_specs,
        out_specs=pl.BlockSpec((None, None, seq, LANE), lambda b, h: (h, b, 0, 0)),
        out_shape=jax.ShapeDtypeStruct((SLABS, bsz, seq, LANE), _BF16),
        scratch_shapes=[pltpu.VMEM((3, seq, LANE), _F32)] * 3,
        compiler_params=pltpu.CompilerParams(
            dimension_semantics=("arbitrary", "arbitrary"), vmem_limit_bytes=VMEM_LIMIT),
        name="attn_b",
    )(slopes, *([proj4] * 10))


def _conv_kernel(cb_ref, cc_ref, cx_ref, cz_ref, w_ref, o_ref):
    u = cc_ref[...].astype(_F32) * cx_ref[...].astype(_F32)
    row = lax.broadcasted_iota(jnp.int32, u.shape, 0)
    u1 = jnp.where(row >= 1, pltpu.roll(u, 1, 0), 0.0)
    u2 = jnp.where(row >= 2, pltpu.roll(u, 2, 0), 0.0)
    w = w_ref[...]
    conv = w[0:1] * u2 + w[1:2] * u1 + w[2:3] * u
    o_ref[...] = (cb_ref[...].astype(_F32) * conv * _silu(cz_ref[...].astype(_F32))).astype(o_ref.dtype)


def _conv(proj4, conv_w):
    bsz, seq = proj4.shape[1], proj4.shape[2]

    def slab(first):
        return pl.BlockSpec((None, None, seq, LANE), lambda b, c: (first + c, b, 0, 0))

    return pl.pallas_call(
        _conv_kernel,
        grid=(bsz, SLABS),
        in_specs=[slab(SLAB_CB), slab(SLAB_CC), slab(SLAB_CX), slab(SLAB_CZ),
                  pl.BlockSpec((C_CONV, LANE), lambda b, c: (0, c))],
        out_specs=pl.BlockSpec((None, None, seq, LANE), lambda b, c: (c, b, 0, 0)),
        out_shape=jax.ShapeDtypeStruct((SLABS, bsz, seq, LANE), _BF16),
        compiler_params=pltpu.CompilerParams(
            dimension_semantics=("arbitrary", "arbitrary"), vmem_limit_bytes=VMEM_LIMIT),
        name="conv",
    )(proj4, proj4, proj4, proj4, conv_w)


def _merge_kernel(ya_ref, yb_ref, yc_ref, gp_ref, gb_ref, x_ref, wa_ref, wb_ref, wc_ref, wo_ref,
                  fg_ref, o_ref, cat_ref, mrg_ref, *, final):
    branch = []
    for y_ref, w_ref in ((ya_ref, wa_ref), (yb_ref, wb_ref), (yc_ref, wc_ref)):
        for c in range(SLABS):
            cat_ref[:, c * LANE:(c + 1) * LANE] = y_ref[c]
        branch.append(jnp.dot(cat_ref[...], w_ref[...], preferred_element_type=_F32))
    per_branch = D_MODEL // LANE
    for c in range(per_branch):
        cols = slice(c * LANE, (c + 1) * LANE)
        tot = None
        for n in range(N_BRANCH):
            gcols = slice(n * D_MODEL + c * LANE, n * D_MODEL + (c + 1) * LANE)
            gate = jax.nn.sigmoid(gp_ref[n * per_branch + c].astype(_F32) + gb_ref[:, gcols])
            term = gate * branch[n][:, cols]
            tot = term if tot is None else tot + term
        mrg_ref[:, cols] = tot.astype(_BF16)
    out = x_ref[...] + jnp.dot(mrg_ref[...], wo_ref[...], preferred_element_type=_F32)
    if final:
        ms = jnp.mean(out * out, axis=-1, keepdims=True)
        out = out * lax.rsqrt(ms + RMS_EPS) * fg_ref[...]
    o_ref[...] = out


def _merge(ya, yb, yc, proj, gate_b, x2d, wa, wb, wc, wo, final_g, *, final, tm=256):
    tokens = x2d.shape[0]
    const = lambda shape: pl.BlockSpec(shape, lambda i: (0,) * len(shape),
                                       pipeline_mode=pl.Buffered(1))
    yspec = pl.BlockSpec((SLABS, tm, LANE), lambda i: (0, i, 0))
    return pl.pallas_call(
        functools.partial(_merge_kernel, final=final),
        grid=(tokens // tm,),
        in_specs=[
            yspec, yspec, yspec,
            pl.BlockSpec((GATE_SLABS, tm, LANE), lambda i: (SLAB_GATE // GATE_SLABS, i, 0)),
            const((1, N_BRANCH * D_MODEL)),
            pl.BlockSpec((tm, D_MODEL), lambda i: (i, 0)),
            const((WIDTH, D_MODEL)), const((WIDTH, D_MODEL)), const((WIDTH, D_MODEL)),
            const((D_MODEL, D_MODEL)),
            const((1, D_MODEL)),
        ],
        out_specs=pl.BlockSpec((tm, D_MODEL), lambda i: (i, 0)),
        out_shape=jax.ShapeDtypeStruct((tokens, D_MODEL), _F32),
        scratch_shapes=[pltpu.VMEM((tm, WIDTH), _BF16), pltpu.VMEM((tm, D_MODEL), _BF16)],
        compiler_params=pltpu.CompilerParams(
            dimension_semantics=("arbitrary",), vmem_limit_bytes=VMEM_LIMIT),
        name="merge",
    )(ya, yb, yc, proj, gate_b.reshape(1, N_BRANCH * D_MODEL), x2d, wa, wb, wc, wo,
      final_g.reshape(1, D_MODEL))


def kernel(x, norm_g, w_in, gate_b, lambda_q1, lambda_k1, lambda_q2, lambda_k2, subln_g, conv_w,
           w_branch_a, w_branch_b, w_branch_c, w_out, final_g):
    bsz, seq, _ = x.shape
    assert seq == SEQ and SLAB_GATE % GATE_SLABS == 0
    tokens = bsz * seq
    slopes = jnp.asarray(2.0 ** (-8.0 * np.arange(1, N_ALIBI + 1) / N_ALIBI), dtype=_F32)
    x2d = x.reshape(tokens, D_MODEL)
    for l in range(DEPTH):
        lam_init = 0.8 - 0.6 * math.exp(-0.3 * l)
        lamv = jnp.stack([lambda_q1[l], lambda_k1[l], lambda_q2[l], lambda_k2[l]])
        proj = _inproj(x2d, norm_g[l], w_in[l].astype(_BF16))
        proj4 = proj.reshape(N_SLABS, bsz, seq, LANE)
        ya = _attn_a(proj4, slopes, lamv, subln_g[l], lam_init=lam_init)
        yb = _attn_b(proj4, slopes)
        yc = _conv(proj4, conv_w[l])
        to3 = lambda y: y.reshape(SLABS, tokens, LANE)
        x2d = _merge(to3(ya), to3(yb), to3(yc), proj, gate_b[l], x2d,
                     w_branch_a[l].astype(_BF16), w_branch_b[l].astype(_BF16),
                     w_branch_c[l].astype(_BF16), w_out[l].astype(_BF16), final_g,
                     final=(l == DEPTH - 1))
    return x2d.reshape(bsz, seq, D_MODEL)
```

```python
import functools
import math

import jax
import jax.numpy as jnp
import numpy as np
from jax import lax
from jax.experimental import pallas as pl
from jax.experimental.pallas import tpu as pltpu

D_MODEL = 2048
SEQ = 2048
DEPTH = 2
A_HEADS = 8
A_HEAD_DIM = 64
B_HEADS = 8
B_HEAD_DIM = 128
B_GROUPS = ((128, 1), (512, 4), (2048, 16))
WIDTH = 1024
C_CONV = 3
N_BRANCH = 3
Q_BLOCK = 128
N_ALIBI = A_HEADS + len(B_GROUPS) * B_HEADS
RMS_EPS = 1e-6
N_IN = 4 * WIDTH + 9 * WIDTH + WIDTH + 4 * WIDTH + N_BRANCH * D_MODEL

LANE = 128
N_SLABS = N_IN // LANE
SLABS = WIDTH // LANE
SLAB_AQ, SLAB_AK, SLAB_AV, SLAB_AZ = 0, SLABS, 2 * SLABS, 3 * SLABS
SLAB_B = 4 * SLABS
SLAB_BZ = SLAB_B + 9 * SLABS
SLAB_CB = SLAB_BZ + SLABS
SLAB_CC, SLAB_CX, SLAB_CZ = SLAB_CB + SLABS, SLAB_CB + 2 * SLABS, SLAB_CB + 3 * SLABS
SLAB_GATE = SLAB_CB + 4 * SLABS
GATE_SLABS = N_BRANCH * D_MODEL // LANE

NEG_BIG = -1e30
VMEM_LIMIT = 56 * 1024 * 1024

_BF16 = jnp.bfloat16
_F32 = jnp.float32


def _silu(z):
    return z * jax.nn.sigmoid(z)


def _rmsnorm_kernel(x_ref, g_ref, o_ref):
    x = x_ref[...]
    ms = jnp.mean(x * x, axis=-1, keepdims=True)
    o_ref[...] = (x * lax.rsqrt(ms + RMS_EPS) * g_ref[...]).astype(o_ref.dtype)


def _rmsnorm(x2d, g, *, tm=1024):
    tokens = x2d.shape[0]
    return pl.pallas_call(
        _rmsnorm_kernel,
        grid=(tokens // tm,),
        in_specs=[pl.BlockSpec((tm, D_MODEL), lambda i: (i, 0)),
                  pl.BlockSpec((1, D_MODEL), lambda i: (0, 0))],
        out_specs=pl.BlockSpec((tm, D_MODEL), lambda i: (i, 0)),
        out_shape=jax.ShapeDtypeStruct((tokens, D_MODEL), _BF16),
        compiler_params=pltpu.CompilerParams(
            dimension_semantics=("arbitrary",), vmem_limit_bytes=VMEM_LIMIT),
        name="rmsnorm",
    )(x2d, g.reshape(1, D_MODEL))


def _inproj_kernel(h_ref, w_ref, o_ref, wb_ref):
    @pl.when(pl.program_id(1) == 0)
    def _():
        wb_ref[...] = w_ref[...].astype(_BF16)

    acc = jnp.dot(h_ref[...], wb_ref[...], preferred_element_type=_F32)
    for c in range(o_ref.shape[0]):
        o_ref[c] = acc[:, c * LANE:(c + 1) * LANE].astype(o_ref.dtype)


def _inproj(h, w_in, layer, *, tm=1024, tn=1024):
    tokens = h.shape[0]
    return pl.pallas_call(
        _inproj_kernel,
        grid=(N_IN // tn, tokens // tm),
        in_specs=[
            pl.BlockSpec((tm, D_MODEL), lambda j, i: (i, 0)),
            pl.BlockSpec((None, D_MODEL, tn), lambda j, i: (layer, 0, j)),
        ],
        out_specs=pl.BlockSpec((tn // LANE, tm, LANE), lambda j, i: (j, i, 0)),
        out_shape=jax.ShapeDtypeStruct((N_SLABS, tokens, LANE), _BF16),
        scratch_shapes=[pltpu.VMEM((D_MODEL, tn), _BF16)],
        compiler_params=pltpu.CompilerParams(
            dimension_semantics=("arbitrary", "arbitrary"), vmem_limit_bytes=VMEM_LIMIT),
        name="inproj",
    )(h, w_in)


def _attn_a_kernel(slopes_ref, lamv_ref, q_ref, k_ref, v_ref, z_ref, sg_ref, o_ref, *,
                   lam_init, tq):
    i = pl.program_id(2)
    slope = slopes_ref[pl.program_id(1)]
    tk = tq

    lane = lax.broadcasted_iota(jnp.int32, (tq, LANE), 1)
    qf = q_ref[...].astype(_F32) * (A_HEAD_DIM ** -0.5)
    qq = jnp.concatenate([jnp.where(lane < A_HEAD_DIM, qf, 0.0),
                          jnp.where(lane >= A_HEAD_DIM, qf, 0.0)], axis=0).astype(_BF16)

    def step(j, carry, masked=False):
        m, l, acc = carry
        start = pl.multiple_of(j * tk, tk)
        s = lax.dot_general(qq, k_ref[pl.ds(start, tk), :], (((1,), (1,)), ((), ())),
                            preferred_element_type=_F32)
        krel = (j * tk - i * tq) + lax.broadcasted_iota(jnp.int32, (1, tk), 1)
        s = s + slope * krel.astype(_F32)
        if masked:
            row = lax.broadcasted_iota(jnp.int32, (2 * tq, tk), 0)
            row = jnp.where(row >= tq, row - tq, row)
            col = lax.broadcasted_iota(jnp.int32, (2 * tq, tk), 1)
            s = jnp.where(col <= row, s, NEG_BIG)
        m_new = jnp.maximum(m, jnp.max(s, axis=-1, keepdims=True))
        alpha = jnp.exp(m - m_new)
        p = jnp.exp(s - m_new)
        l = alpha * l + jnp.sum(p, axis=-1, keepdims=True)
        pv = jnp.dot(p.astype(_BF16), v_ref[pl.ds(start, tk), :], preferred_element_type=_F32)
        return m_new, l, alpha * acc + pv

    init = (jnp.full((2 * tq, 1), NEG_BIG, _F32), jnp.zeros((2 * tq, 1), _F32),
            jnp.zeros((2 * tq, LANE), _F32))
    _, l, acc = step(i, lax.fori_loop(0, i, step, init), masked=True)

    lv = lamv_ref[...]
    lam = (jnp.exp(jnp.sum(lv[0:1] * lv[1:2], axis=-1, keepdims=True))
           - jnp.exp(jnp.sum(lv[2:3] * lv[3:4], axis=-1, keepdims=True)) + lam_init)
    o = acc / l
    out = o[:tq] - lam * o[tq:]
    ms = jnp.mean(out * out, axis=-1, keepdims=True)
    oa = out * lax.rsqrt(ms + RMS_EPS) * sg_ref[...] * (1.0 - lam_init)
    o_ref[...] = (oa * _silu(z_ref[...].astype(_F32))).astype(o_ref.dtype)


def _attn_a(proj4, slopes, lamv, subln_g, *, lam_init, tq=512):
    bsz, seq = proj4.shape[1], proj4.shape[2]
    return pl.pallas_call(
        functools.partial(_attn_a_kernel, lam_init=lam_init, tq=tq),
        grid=(bsz, A_HEADS, seq // tq),
        in_specs=[
            pl.BlockSpec(memory_space=pltpu.SMEM),
            pl.BlockSpec((4, A_HEAD_DIM), lambda b, h, i: (0, 0)),
            pl.BlockSpec((None, None, tq, LANE), lambda b, h, i: (SLAB_AQ + h, b, i, 0)),
            pl.BlockSpec((None, None, seq, LANE), lambda b, h, i: (SLAB_AK + h, b, 0, 0)),
            pl.BlockSpec((None, None, seq, LANE), lambda b, h, i: (SLAB_AV + h, b, 0, 0)),
            pl.BlockSpec((None, None, tq, LANE), lambda b, h, i: (SLAB_AZ + h, b, i, 0)),
            pl.BlockSpec((1, LANE), lambda b, h, i: (0, 0)),
        ],
        out_specs=pl.BlockSpec((None, None, tq, LANE), lambda b, h, i: (h, b, i, 0)),
        out_shape=jax.ShapeDtypeStruct((SLABS, bsz, seq, LANE), _BF16),
        compiler_params=pltpu.CompilerParams(
            dimension_semantics=("arbitrary", "arbitrary", "arbitrary"),
            vmem_limit_bytes=VMEM_LIMIT),
        name="attn_a",
    )(slopes, lamv, proj4, proj4, proj4, proj4, subln_g.reshape(1, LANE))


def _band_bias(slope, dil):
    qi = lax.broadcasted_iota(jnp.int32, (Q_BLOCK, 2 * Q_BLOCK), 0)
    ki = lax.broadcasted_iota(jnp.int32, (Q_BLOCK, 2 * Q_BLOCK), 1)
    off = Q_BLOCK + qi - ki
    valid = (off >= 0) & (off <= Q_BLOCK)
    return jnp.where(valid, -(slope * dil) * off.astype(_F32), NEG_BIG)


def _band_block(q, k, v, bias):
    s = lax.dot_general(q, k, (((1,), (1,)), ((), ())), preferred_element_type=_F32)
    s = s * (B_HEAD_DIM ** -0.5) + bias
    m = jnp.max(s, axis=-1, keepdims=True)
    p = jnp.exp(s - m)
    l = jnp.sum(p, axis=-1, keepdims=True)
    o = jnp.dot(p.astype(_BF16), v, preferred_element_type=_F32) / l
    return o, m + jnp.log(l)


def _attn_b_kernel(slopes_ref, q1_ref, k1_ref, v1_ref, q2_ref, k2_ref, v2_ref,
                   q3_ref, k3_ref, v3_ref, z_ref, o_ref, f_ref, og_ref, lg_ref):
    seq = o_ref.shape[0]
    h = pl.program_id(1)
    group_refs = ((q1_ref, k1_ref, v1_ref), (q2_ref, k2_ref, v2_ref), (q3_ref, k3_ref, v3_ref))

    for g, (window, dil) in enumerate(B_GROUPS):
        assert window // dil == Q_BLOCK
        slope = slopes_ref[A_HEADS + g * B_HEADS + h]
        bias = _band_bias(slope, dil)
        bias0 = bias[:, Q_BLOCK:]
        qr, kr, vr = group_refs[g]
        sub = seq // dil
        if dil > 1:
            f_ref[0] = qr[...].astype(_F32)
            f_ref[1] = kr[...].astype(_F32)
            f_ref[2] = vr[...].astype(_F32)
        for r in range(dil):
            for n in range(sub // Q_BLOCK):
                first = n == 0
                k0 = n * Q_BLOCK if first else (n - 1) * Q_BLOCK
                nk = Q_BLOCK if first else 2 * Q_BLOCK
                if dil == 1:
                    q_rows = pl.ds(n * Q_BLOCK, Q_BLOCK)
                    q, k, v = qr[q_rows, :], kr[pl.ds(k0, nk), :], vr[pl.ds(k0, nk), :]
                else:
                    q_rows = pl.ds(n * Q_BLOCK * dil + r, Q_BLOCK, stride=dil)
                    k_rows = pl.ds(k0 * dil + r, nk, stride=dil)
                    q = f_ref[0, q_rows, :].astype(_BF16)
                    k = f_ref[1, k_rows, :].astype(_BF16)
                    v = f_ref[2, k_rows, :].astype(_BF16)
                o, lse = _band_block(q, k, v, bias0 if first else bias)
                og_ref[g, q_rows, :] = o
                lg_ref[g, q_rows, :] = jnp.broadcast_to(lse, (Q_BLOCK, LANE))

    rows = 256
    for c in range(seq // rows):
        sl = pl.ds(c * rows, rows)
        l0, l1, l2 = lg_ref[0, sl, :], lg_ref[1, sl, :], lg_ref[2, sl, :]
        mx = jnp.maximum(jnp.maximum(l0, l1), l2)
        w0, w1, w2 = jnp.exp(l0 - mx), jnp.exp(l1 - mx), jnp.exp(l2 - mx)
        ob = (w0 * og_ref[0, sl, :] + w1 * og_ref[1, sl, :] + w2 * og_ref[2, sl, :]) / (w0 + w1 + w2)
        o_ref[sl, :] = (ob * _silu(z_ref[sl, :].astype(_F32))).astype(o_ref.dtype)


def _attn_b(proj4, slopes):
    bsz, seq = proj4.shape[1], proj4.shape[2]

    def slab(first):
        return pl.BlockSpec((None, None, seq, LANE), lambda b, h: (first + h, b, 0, 0))

    in_specs = [pl.BlockSpec(memory_space=pltpu.SMEM)]
    for g in range(len(B_GROUPS)):
        for part in range(3):
            in_specs.append(slab(SLAB_B + (3 * g + part) * SLABS))
    in_specs.append(slab(SLAB_BZ))
    return pl.pallas_call(
        _attn_b_kernel,
        grid=(bsz, B_HEADS),
        in_specs=in_specs,
        out_specs=pl.BlockSpec((None, None, seq, LANE), lambda b, h: (h, b, 0, 0)),
        out_shape=jax.ShapeDtypeStruct((SLABS, bsz, seq, LANE), _BF16),
        scratch_shapes=[pltpu.VMEM((3, seq, LANE), _F32)] * 3,
        compiler_params=pltpu.CompilerParams(
            dimension_semantics=("arbitrary", "arbitrary"), vmem_limit_bytes=VMEM_LIMIT),
        name="attn_b",
    )(slopes, *([proj4] * 10))


def _conv_kernel(cb_ref, cc_ref, cx_ref, cz_ref, w_ref, o_ref):
    u = cc_ref[...].astype(_F32) * cx_ref[...].astype(_F32)
    row = lax.broadcasted_iota(jnp.int32, u.shape, 0)
    u1 = jnp.where(row >= 1, pltpu.roll(u, 1, 0), 0.0)
    u2 = jnp.where(row >= 2, pltpu.roll(u, 2, 0), 0.0)
    w = w_ref[...]
    conv = w[0:1] * u2 + w[1:2] * u1 + w[2:3] * u
    o_ref[...] = (cb_ref[...].astype(_F32) * conv * _silu(cz_ref[...].astype(_F32))).astype(o_ref.dtype)


def _conv(proj4, conv_w):
    bsz, seq = proj4.shape[1], proj4.shape[2]

    def slab(first):
        return pl.BlockSpec((None, None, seq, LANE), lambda b, c: (first + c, b, 0, 0))

    return pl.pallas_call(
        _conv_kernel,
        grid=(bsz, SLABS),
        in_specs=[slab(SLAB_CB), slab(SLAB_CC), slab(SLAB_CX), slab(SLAB_CZ),
                  pl.BlockSpec((C_CONV, LANE), lambda b, c: (0, c))],
        out_specs=pl.BlockSpec((None, None, seq, LANE), lambda b, c: (c, b, 0, 0)),
        out_shape=jax.ShapeDtypeStruct((SLABS, bsz, seq, LANE), _BF16),
        compiler_params=pltpu.CompilerParams(
            dimension_semantics=("arbitrary", "arbitrary"), vmem_limit_bytes=VMEM_LIMIT),
        name="conv",
    )(proj4, proj4, proj4, proj4, conv_w)


def _merge_kernel(ya_ref, yb_ref, yc_ref, gp_ref, gb_ref, x_ref, wa_ref, wb_ref, wc_ref, wo_ref,
                  ng_ref, *rest, final):
    if final:
        o_ref, cat_ref, mrg_ref = rest
    else:
        o_ref, h_ref, cat_ref, mrg_ref = rest
    branch = []
    for y_ref, w_ref in ((ya_ref, wa_ref), (yb_ref, wb_ref), (yc_ref, wc_ref)):
        for c in range(SLABS):
            cat_ref[:, c * LANE:(c + 1) * LANE] = y_ref[c]
        branch.append(jnp.dot(cat_ref[...], w_ref[...], preferred_element_type=_F32))
    per_branch = D_MODEL // LANE
    for c in range(per_branch):
        cols = slice(c * LANE, (c + 1) * LANE)
        tot = None
        for n in range(N_BRANCH):
            gcols = slice(n * D_MODEL + c * LANE, n * D_MODEL + (c + 1) * LANE)
            gate = jax.nn.sigmoid(gp_ref[n * per_branch + c].astype(_F32) + gb_ref[:, gcols])
            term = gate * branch[n][:, cols]
            tot = term if tot is None else tot + term
        mrg_ref[:, cols] = tot.astype(_BF16)
    out = x_ref[...] + jnp.dot(mrg_ref[...], wo_ref[...], preferred_element_type=_F32)
    ms = jnp.mean(out * out, axis=-1, keepdims=True)
    normed = out * lax.rsqrt(ms + RMS_EPS) * ng_ref[...]
    if final:
        o_ref[...] = normed
    else:
        o_ref[...] = out
        h_ref[...] = normed.astype(h_ref.dtype)


def _merge(ya, yb, yc, proj, gate_b, x2d, wa, wb, wc, wo, next_g, *, final, tm=256):
    tokens = x2d.shape[0]
    row_spec = pl.BlockSpec((tm, D_MODEL), lambda i: (i, 0))
    if final:
        out_specs, out_shape = row_spec, jax.ShapeDtypeStruct((tokens, D_MODEL), _F32)
    else:
        out_specs = (row_spec, row_spec)
        out_shape = (jax.ShapeDtypeStruct((tokens, D_MODEL), _F32),
                     jax.ShapeDtypeStruct((tokens, D_MODEL), _BF16))
    const = lambda shape: pl.BlockSpec(shape, lambda i: (0,) * len(shape),
                                       pipeline_mode=pl.Buffered(1))
    yspec = pl.BlockSpec((SLABS, tm, LANE), lambda i: (0, i, 0))
    return pl.pallas_call(
        functools.partial(_merge_kernel, final=final),
        grid=(tokens // tm,),
        in_specs=[
            yspec, yspec, yspec,
            pl.BlockSpec((GATE_SLABS, tm, LANE), lambda i: (SLAB_GATE // GATE_SLABS, i, 0)),
            const((1, N_BRANCH * D_MODEL)),
            pl.BlockSpec((tm, D_MODEL), lambda i: (i, 0)),
            const((WIDTH, D_MODEL)), const((WIDTH, D_MODEL)), const((WIDTH, D_MODEL)),
            const((D_MODEL, D_MODEL)),
            const((1, D_MODEL)),
        ],
        out_specs=out_specs,
        out_shape=out_shape,
        scratch_shapes=[pltpu.VMEM((tm, WIDTH), _BF16), pltpu.VMEM((tm, D_MODEL), _BF16)],
        compiler_params=pltpu.CompilerParams(
            dimension_semantics=("arbitrary",), vmem_limit_bytes=VMEM_LIMIT),
        name="merge",
    )(ya, yb, yc, proj, gate_b.reshape(1, N_BRANCH * D_MODEL), x2d, wa, wb, wc, wo,
      next_g.reshape(1, D_MODEL))


def kernel(x, norm_g, w_in, gate_b, lambda_q1, lambda_k1, lambda_q2, lambda_k2, subln_g, conv_w,
           w_branch_a, w_branch_b, w_branch_c, w_out, final_g):
    bsz, seq, _ = x.shape
    assert seq == SEQ and SLAB_GATE % GATE_SLABS == 0
    tokens = bsz * seq
    slopes = jnp.asarray(2.0 ** (-8.0 * np.arange(1, N_ALIBI + 1) / N_ALIBI), dtype=_F32)
    x2d = x.reshape(tokens, D_MODEL)
    h = _rmsnorm(x2d, norm_g[0])
    for l in range(DEPTH):
        final = l == DEPTH - 1
        lam_init = 0.8 - 0.6 * math.exp(-0.3 * l)
        lamv = jnp.stack([lambda_q1[l], lambda_k1[l], lambda_q2[l], lambda_k2[l]])
        proj = _inproj(h, w_in, l)
        proj4 = proj.reshape(N_SLABS, bsz, seq, LANE)
        ya = _attn_a(proj4, slopes, lamv, subln_g[l], lam_init=lam_init)
        yb = _attn_b(proj4, slopes)
        yc = _conv(proj4, conv_w[l])
        to3 = lambda y: y.reshape(SLABS, tokens, LANE)
        res = _merge(to3(ya), to3(yb), to3(yc), proj, gate_b[l], x2d,
                     w_branch_a[l].astype(_BF16), w_branch_b[l].astype(_BF16),
                     w_branch_c[l].astype(_BF16), w_out[l].astype(_BF16),
                     final_g if final else norm_g[l + 1], final=final)
        x2d, h = (res, None) if final else res
    return x2d.reshape(bsz, seq, D_MODEL)
```

```python
import functools
import math

import jax
import jax.numpy as jnp
import numpy as np
from jax import lax
from jax.experimental import pallas as pl
from jax.experimental.pallas import tpu as pltpu

D_MODEL = 2048
SEQ = 2048
DEPTH = 2
A_HEADS = 8
A_HEAD_DIM = 64
B_HEADS = 8
B_HEAD_DIM = 128
B_GROUPS = ((128, 1), (512, 4), (2048, 16))
WIDTH = 1024
C_CONV = 3
N_BRANCH = 3
Q_BLOCK = 128
N_ALIBI = A_HEADS + len(B_GROUPS) * B_HEADS
RMS_EPS = 1e-6
N_IN = 4 * WIDTH + 9 * WIDTH + WIDTH + 4 * WIDTH + N_BRANCH * D_MODEL

LANE = 128
N_SLABS = N_IN // LANE
SLABS = WIDTH // LANE
SLAB_AQ, SLAB_AK, SLAB_AV, SLAB_AZ = 0, SLABS, 2 * SLABS, 3 * SLABS
SLAB_B = 4 * SLABS
SLAB_BZ = SLAB_B + 9 * SLABS
SLAB_CB = SLAB_BZ + SLABS
SLAB_CC, SLAB_CX, SLAB_CZ = SLAB_CB + SLABS, SLAB_CB + 2 * SLABS, SLAB_CB + 3 * SLABS
SLAB_GATE = SLAB_CB + 4 * SLABS
GATE_SLABS = N_BRANCH * D_MODEL // LANE

NEG_BIG = -1e30
LOG2E = 1.4426950408889634
VMEM_LIMIT = 56 * 1024 * 1024

_BF16 = jnp.bfloat16
_F32 = jnp.float32


def _silu(z):
    return z * jax.nn.sigmoid(z)


def _rmsnorm_kernel(x_ref, g_ref, o_ref):
    x = x_ref[...]
    ms = jnp.mean(x * x, axis=-1, keepdims=True)
    o_ref[...] = (x * lax.rsqrt(ms + RMS_EPS) * g_ref[...]).astype(o_ref.dtype)


def _rmsnorm(x2d, g, *, tm=1024):
    tokens = x2d.shape[0]
    return pl.pallas_call(
        _rmsnorm_kernel,
        grid=(tokens // tm,),
        in_specs=[pl.BlockSpec((tm, D_MODEL), lambda i: (i, 0)),
                  pl.BlockSpec((1, D_MODEL), lambda i: (0, 0))],
        out_specs=pl.BlockSpec((tm, D_MODEL), lambda i: (i, 0)),
        out_shape=jax.ShapeDtypeStruct((tokens, D_MODEL), _BF16),
        compiler_params=pltpu.CompilerParams(
            dimension_semantics=("arbitrary",), vmem_limit_bytes=VMEM_LIMIT),
        name="rmsnorm",
    )(x2d, g.reshape(1, D_MODEL))


def _inproj_kernel(h_ref, w_ref, o_ref, wb_ref):
    @pl.when(pl.program_id(1) == 0)
    def _():
        wb_ref[...] = w_ref[...].astype(_BF16)

    acc = jnp.dot(h_ref[...], wb_ref[...], preferred_element_type=_F32)
    for c in range(o_ref.shape[0]):
        o_ref[c] = acc[:, c * LANE:(c + 1) * LANE].astype(o_ref.dtype)


def _inproj(h, w_in, layer, *, tm=1024, tn=1024):
    tokens = h.shape[0]
    return pl.pallas_call(
        _inproj_kernel,
        grid=(N_IN // tn, tokens // tm),
        in_specs=[
            pl.BlockSpec((tm, D_MODEL), lambda j, i: (i, 0)),
            pl.BlockSpec((None, D_MODEL, tn), lambda j, i: (layer, 0, j)),
        ],
        out_specs=pl.BlockSpec((tn // LANE, tm, LANE), lambda j, i: (j, i, 0)),
        out_shape=jax.ShapeDtypeStruct((N_SLABS, tokens, LANE), _BF16),
        scratch_shapes=[pltpu.VMEM((D_MODEL, tn), _BF16)],
        compiler_params=pltpu.CompilerParams(
            dimension_semantics=("arbitrary", "arbitrary"), vmem_limit_bytes=VMEM_LIMIT),
        name="inproj",
    )(h, w_in)


ALIBI_PIECES = 3


def _attn_a_kernel(lamv_ref, q_ref, k_ref, v_ref, z_ref, bias_ref, sg_ref, o_ref, kk_ref, va_ref,
                   *, lam_init, tq, heads):
    seq = q_ref.shape[1]
    tk = tq
    lane_s = lax.broadcasted_iota(jnp.int32, (seq, LANE), 1)
    ones_col = jnp.where(lane_s == 0, 1.0, 0.0).astype(_BF16)
    for g in range(heads):
        kk_ref[g, :, :LANE] = k_ref[g]
        kk_ref[g, :, LANE:] = bias_ref[g]
        va_ref[g, :, :LANE] = v_ref[g]
        va_ref[g, :, LANE:] = ones_col

    lane = lax.broadcasted_iota(jnp.int32, (tq, LANE), 1)
    lane2 = lax.broadcasted_iota(jnp.int32, (2 * tq, LANE), 1)
    q_ones = jnp.where(lane2 < ALIBI_PIECES, 1.0, 0.0).astype(_BF16)
    row = lax.broadcasted_iota(jnp.int32, (2 * tq, tk), 0)
    causal = lax.broadcasted_iota(jnp.int32, (2 * tq, tk), 1) <= jnp.where(row >= tq, row - tq, row)
    lv = lamv_ref[...]
    lam = (jnp.exp(jnp.sum(lv[0:1] * lv[1:2], axis=-1, keepdims=True))
           - jnp.exp(jnp.sum(lv[2:3] * lv[3:4], axis=-1, keepdims=True)) + lam_init)

    def stacked_q(g, i):
        qf = q_ref[g, pl.ds(i * tq, tq), :].astype(_F32) * (A_HEAD_DIM ** -0.5)
        qq = jnp.concatenate([jnp.where(lane < A_HEAD_DIM, qf, 0.0),
                              jnp.where(lane >= A_HEAD_DIM, qf, 0.0)], axis=0).astype(_BF16)
        return jnp.concatenate([qq, q_ones], axis=1)

    def scores(qqs, i, j):
        out = []
        for g in range(heads):
            s = lax.dot_general(qqs[g], kk_ref[g, pl.ds(j * tk, tk), :], (((1,), (1,)), ((), ())),
                                preferred_element_type=_F32)
            out.append(jnp.where(causal, s, NEG_BIG) if j == i else s)
        return out

    def finish(i, carries):
        for g in range(heads):
            acc = carries[g][1]
            o = acc[:, :LANE] / acc[:, LANE:LANE + 1]
            out = o[:tq] - lam * o[tq:]
            ms = jnp.mean(out * out, axis=-1, keepdims=True)
            oa = out * lax.rsqrt(ms + RMS_EPS) * sg_ref[...] * (1.0 - lam_init)
            rows = pl.ds(i * tq, tq)
            o_ref[g, rows, :] = (oa * _silu(z_ref[g, rows, :].astype(_F32))).astype(o_ref.dtype)

    steps = [(i, j) for i in range(seq // tq) for j in range(i + 1)]
    init = (jnp.full((2 * tq, 1), NEG_BIG, _F32), jnp.zeros((2 * tq, 2 * LANE), _F32))
    qqs = [stacked_q(g, 0) for g in range(heads)]
    s_next = scores(qqs, 0, 0)
    carries = [init] * heads
    for t, (i, j) in enumerate(steps):
        s_cur = s_next
        if t + 1 < len(steps):
            ni, nj = steps[t + 1]
            if nj == 0:
                qqs = [stacked_q(g, ni) for g in range(heads)]
            s_next = scores(qqs, ni, nj)
        probs, m_news = [], []
        for g in range(heads):
            m_new = jnp.maximum(carries[g][0], jnp.max(s_cur[g], axis=-1, keepdims=True))
            probs.append(jnp.exp((s_cur[g] - m_new).astype(_BF16)))
            m_news.append(m_new)
        new = []
        for g in range(heads):
            m, acc = carries[g]
            pv = jnp.dot(probs[g], va_ref[g, pl.ds(j * tk, tk), :], preferred_element_type=_F32)
            new.append((m_news[g], jnp.exp(m - m_news[g]) * acc + pv))
        carries = new
        if j == i:
            finish(i, carries)
            carries = [init] * heads


def _alibi_table(slopes, seq):
    rest = slopes[:A_HEADS, None] * jnp.arange(seq, dtype=_F32)[None, :]
    pieces = []
    for _ in range(ALIBI_PIECES):
        bits = lax.bitcast_convert_type(rest, jnp.uint32) & jnp.uint32(0xFFFF0000)
        part = lax.bitcast_convert_type(bits, _F32)
        pieces.append(part.astype(_BF16))
        rest = rest - part
    tab = jnp.stack(pieces, axis=-1)
    return jnp.pad(tab, ((0, 0), (0, 0), (0, LANE - ALIBI_PIECES)))


def _attn_a(proj4, slopes, lamv, subln_g, *, lam_init, tq=512, heads=2):
    bsz, seq = proj4.shape[1], proj4.shape[2]

    def slabs(first):
        return pl.BlockSpec((heads, None, seq, LANE), lambda b, h: (first // heads + h, b, 0, 0))

    return pl.pallas_call(
        functools.partial(_attn_a_kernel, lam_init=lam_init, tq=tq, heads=heads),
        grid=(bsz, A_HEADS // heads),
        in_specs=[
            pl.BlockSpec((4, A_HEAD_DIM), lambda b, h: (0, 0)),
            slabs(SLAB_AQ), slabs(SLAB_AK), slabs(SLAB_AV), slabs(SLAB_AZ),
            pl.BlockSpec((heads, seq, LANE), lambda b, h: (h, 0, 0)),
            pl.BlockSpec((1, LANE), lambda b, h: (0, 0)),
        ],
        out_specs=slabs(0),
        out_shape=jax.ShapeDtypeStruct((SLABS, bsz, seq, LANE), _BF16),
        scratch_shapes=[pltpu.VMEM((heads, seq, 2 * LANE), _BF16)] * 2,
        compiler_params=pltpu.CompilerParams(
            dimension_semantics=("arbitrary", "arbitrary"), vmem_limit_bytes=VMEM_LIMIT),
        name="attn_a",
    )(lamv, proj4, proj4, proj4, proj4, _alibi_table(slopes, seq), subln_g.reshape(1, LANE))


def _band_bias(slope, dil):
    qi = lax.broadcasted_iota(jnp.int32, (Q_BLOCK, 2 * Q_BLOCK), 0)
    ki = lax.broadcasted_iota(jnp.int32, (Q_BLOCK, 2 * Q_BLOCK), 1)
    off = Q_BLOCK + qi - ki
    valid = (off >= 0) & (off <= Q_BLOCK)
    return jnp.where(valid, -(slope * dil) * off.astype(_F32), NEG_BIG)


def _band_block(q, k, v, bias):
    s = lax.dot_general(q, k, (((1,), (1,)), ((), ())), preferred_element_type=_F32)
    s = s * (B_HEAD_DIM ** -0.5) + bias
    m = jnp.max(s, axis=-1, keepdims=True)
    p = jnp.exp(s - m)
    l = jnp.sum(p, axis=-1, keepdims=True)
    o = jnp.dot(p.astype(_BF16), v, preferred_element_type=_F32) / l
    return o, m + jnp.log(l)


def _attn_b_kernel(slopes_ref, q1_ref, k1_ref, v1_ref, q2_ref, k2_ref, v2_ref,
                   q3_ref, k3_ref, v3_ref, z_ref, o_ref, f_ref, og_ref, lg_ref):
    seq = o_ref.shape[0]
    h = pl.program_id(1)
    group_refs = ((q1_ref, k1_ref, v1_ref), (q2_ref, k2_ref, v2_ref), (q3_ref, k3_ref, v3_ref))

    for g, (window, dil) in enumerate(B_GROUPS):
        assert window // dil == Q_BLOCK
        slope = slopes_ref[A_HEADS + g * B_HEADS + h]
        bias = _band_bias(slope, dil)
        bias0 = bias[:, Q_BLOCK:]
        qr, kr, vr = group_refs[g]
        sub = seq // dil
        if dil > 1:
            f_ref[0] = qr[...].astype(_F32)
            f_ref[1] = kr[...].astype(_F32)
            f_ref[2] = vr[...].astype(_F32)
        for r in range(dil):
            for n in range(sub // Q_BLOCK):
                first = n == 0
                k0 = n * Q_BLOCK if first else (n - 1) * Q_BLOCK
                nk = Q_BLOCK if first else 2 * Q_BLOCK
                if dil == 1:
                    q_rows = pl.ds(n * Q_BLOCK, Q_BLOCK)
                    q, k, v = qr[q_rows, :], kr[pl.ds(k0, nk), :], vr[pl.ds(k0, nk), :]
                else:
                    q_rows = pl.ds(n * Q_BLOCK * dil + r, Q_BLOCK, stride=dil)
                    k_rows = pl.ds(k0 * dil + r, nk, stride=dil)
                    q = f_ref[0, q_rows, :].astype(_BF16)
                    k = f_ref[1, k_rows, :].astype(_BF16)
                    v = f_ref[2, k_rows, :].astype(_BF16)
                o, lse = _band_block(q, k, v, bias0 if first else bias)
                og_ref[g, q_rows, :] = o
                lg_ref[g, q_rows, :] = jnp.broadcast_to(lse, (Q_BLOCK, LANE))

    rows = 256
    for c in range(seq // rows):
        sl = pl.ds(c * rows, rows)
        l0, l1, l2 = lg_ref[0, sl, :], lg_ref[1, sl, :], lg_ref[2, sl, :]
        mx = jnp.maximum(jnp.maximum(l0, l1), l2)
        w0, w1, w2 = jnp.exp(l0 - mx), jnp.exp(l1 - mx), jnp.exp(l2 - mx)
        ob = (w0 * og_ref[0, sl, :] + w1 * og_ref[1, sl, :] + w2 * og_ref[2, sl, :]) / (w0 + w1 + w2)
        o_ref[sl, :] = (ob * _silu(z_ref[sl, :].astype(_F32))).astype(o_ref.dtype)


def _attn_b(proj4, slopes):
    bsz, seq = proj4.shape[1], proj4.shape[2]

    def slab(first):
        return pl.BlockSpec((None, None, seq, LANE), lambda b, h: (first + h, b, 0, 0))

    in_specs = [pl.BlockSpec(memory_space=pltpu.SMEM)]
    for g in range(len(B_GROUPS)):
        for part in range(3):
            in_specs.append(slab(SLAB_B + (3 * g + part) * SLABS))
    in_specs.append(slab(SLAB_BZ))
    return pl.pallas_call(
        _attn_b_kernel,
        grid=(bsz, B_HEADS),
        in_specs=in_specs,
        out_specs=pl.BlockSpec((None, None, seq, LANE), lambda b, h: (h, b, 0, 0)),
        out_shape=jax.ShapeDtypeStruct((SLABS, bsz, seq, LANE), _BF16),
        scratch_shapes=[pltpu.VMEM((3, seq, LANE), _F32)] * 3,
        compiler_params=pltpu.CompilerParams(
            dimension_semantics=("arbitrary", "arbitrary"), vmem_limit_bytes=VMEM_LIMIT),
        name="attn_b",
    )(slopes, *([proj4] * 10))


def _conv_kernel(cb_ref, cc_ref, cx_ref, cz_ref, w_ref, o_ref):
    u = cc_ref[...].astype(_F32) * cx_ref[...].astype(_F32)
    row = lax.broadcasted_iota(jnp.int32, u.shape, 0)
    u1 = jnp.where(row >= 1, pltpu.roll(u, 1, 0), 0.0)
    u2 = jnp.where(row >= 2, pltpu.roll(u, 2, 0), 0.0)
    w = w_ref[...]
    conv = w[0:1] * u2 + w[1:2] * u1 + w[2:3] * u
    o_ref[...] = (cb_ref[...].astype(_F32) * conv * _silu(cz_ref[...].astype(_F32))).astype(o_ref.dtype)


def _conv(proj4, conv_w):
    bsz, seq = proj4.shape[1], proj4.shape[2]

    def slab(first):
        return pl.BlockSpec((None, None, seq, LANE), lambda b, c: (first + c, b, 0, 0))

    return pl.pallas_call(
        _conv_kernel,
        grid=(bsz, SLABS),
        in_specs=[slab(SLAB_CB), slab(SLAB_CC), slab(SLAB_CX), slab(SLAB_CZ),
                  pl.BlockSpec((C_CONV, LANE), lambda b, c: (0, c))],
        out_specs=pl.BlockSpec((None, None, seq, LANE), lambda b, c: (c, b, 0, 0)),
        out_shape=jax.ShapeDtypeStruct((SLABS, bsz, seq, LANE), _BF16),
        compiler_params=pltpu.CompilerParams(
            dimension_semantics=("arbitrary", "arbitrary"), vmem_limit_bytes=VMEM_LIMIT),
        name="conv",
    )(proj4, proj4, proj4, proj4, conv_w)


def _merge_kernel(ya_ref, yb_ref, yc_ref, gp_ref, gb_ref, x_ref, wa_ref, wb_ref, wc_ref, wo_ref,
                  ng_ref, *rest, final):
    if final:
        o_ref, cat_ref, mrg_ref = rest
    else:
        o_ref, h_ref, cat_ref, mrg_ref = rest
    branch = []
    for y_ref, w_ref in ((ya_ref, wa_ref), (yb_ref, wb_ref), (yc_ref, wc_ref)):
        for c in range(SLABS):
            cat_ref[:, c * LANE:(c + 1) * LANE] = y_ref[c]
        branch.append(jnp.dot(cat_ref[...], w_ref[...], preferred_element_type=_F32))
    per_branch = D_MODEL // LANE
    for c in range(per_branch):
        cols = slice(c * LANE, (c + 1) * LANE)
        tot = None
        for n in range(N_BRANCH):
            gcols = slice(n * D_MODEL + c * LANE, n * D_MODEL + (c + 1) * LANE)
            gate = jax.nn.sigmoid(gp_ref[n * per_branch + c].astype(_F32) + gb_ref[:, gcols])
            term = gate * branch[n][:, cols]
            tot = term if tot is None else tot + term
        mrg_ref[:, cols] = tot.astype(_BF16)
    out = x_ref[...] + jnp.dot(mrg_ref[...], wo_ref[...], preferred_element_type=_F32)
    ms = jnp.mean(out * out, axis=-1, keepdims=True)
    normed = out * lax.rsqrt(ms + RMS_EPS) * ng_ref[...]
    if final:
        o_ref[...] = normed
    else:
        o_ref[...] = out
        h_ref[...] = normed.astype(h_ref.dtype)


def _merge(ya, yb, yc, proj, gate_b, x2d, wa, wb, wc, wo, next_g, *, final, tm=256):
    tokens = x2d.shape[0]
    row_spec = pl.BlockSpec((tm, D_MODEL), lambda i: (i, 0))
    if final:
        out_specs, out_shape = row_spec, jax.ShapeDtypeStruct((tokens, D_MODEL), _F32)
    else:
        out_specs = (row_spec, row_spec)
        out_shape = (jax.ShapeDtypeStruct((tokens, D_MODEL), _F32),
                     jax.ShapeDtypeStruct((tokens, D_MODEL), _BF16))
    const = lambda shape: pl.BlockSpec(shape, lambda i: (0,) * len(shape),
                                       pipeline_mode=pl.Buffered(1))
    yspec = pl.BlockSpec((SLABS, tm, LANE), lambda i: (0, i, 0))
    return pl.pallas_call(
        functools.partial(_merge_kernel, final=final),
        grid=(tokens // tm,),
        in_specs=[
            yspec, yspec, yspec,
            pl.BlockSpec((GATE_SLABS, tm, LANE), lambda i: (SLAB_GATE // GATE_SLABS, i, 0)),
            const((1, N_BRANCH * D_MODEL)),
            pl.BlockSpec((tm, D_MODEL), lambda i: (i, 0)),
            const((WIDTH, D_MODEL)), const((WIDTH, D_MODEL)), const((WIDTH, D_MODEL)),
            const((D_MODEL, D_MODEL)),
            const((1, D_MODEL)),
        ],
        out_specs=out_specs,
        out_shape=out_shape,
        scratch_shapes=[pltpu.VMEM((tm, WIDTH), _BF16), pltpu.VMEM((tm, D_MODEL), _BF16)],
        compiler_params=pltpu.CompilerParams(
            dimension_semantics=("arbitrary",), vmem_limit_bytes=VMEM_LIMIT),
        name="merge",
    )(ya, yb, yc, proj, gate_b.reshape(1, N_BRANCH * D_MODEL), x2d, wa, wb, wc, wo,
      next_g.reshape(1, D_MODEL))


def kernel(x, norm_g, w_in, gate_b, lambda_q1, lambda_k1, lambda_q2, lambda_k2, subln_g, conv_w,
           w_branch_a, w_branch_b, w_branch_c, w_out, final_g):
    bsz, seq, _ = x.shape
    assert seq == SEQ and SLAB_GATE % GATE_SLABS == 0
    tokens = bsz * seq
    slopes = jnp.asarray(2.0 ** (-8.0 * np.arange(1, N_ALIBI + 1) / N_ALIBI), dtype=_F32)
    x2d = x.reshape(tokens, D_MODEL)
    h = _rmsnorm(x2d, norm_g[0])
    for l in range(DEPTH):
        final = l == DEPTH - 1
        lam_init = 0.8 - 0.6 * math.exp(-0.3 * l)
        lamv = jnp.stack([lambda_q1[l], lambda_k1[l], lambda_q2[l], lambda_k2[l]])
        proj = _inproj(h, w_in, l)
        proj4 = proj.reshape(N_SLABS, bsz, seq, LANE)
        ya = _attn_a(proj4, slopes, lamv, subln_g[l], lam_init=lam_init)
        yb = _attn_b(proj4, slopes)
        yc = _conv(proj4, conv_w[l])
        to3 = lambda y: y.reshape(SLABS, tokens, LANE)
        res = _merge(to3(ya), to3(yb), to3(yc), proj, gate_b[l], x2d,
                     w_branch_a[l].astype(_BF16), w_branch_b[l].astype(_BF16),
                     w_branch_c[l].astype(_BF16), w_out[l].astype(_BF16),
                     final_g if final else norm_g[l + 1], final=final)
        x2d, h = (res, None) if final else res
    return x2d.reshape(bsz, seq, D_MODEL)
```

```python
import functools
import math

import jax
import jax.numpy as jnp
import numpy as np
from jax import lax
from jax.experimental import pallas as pl
from jax.experimental.pallas import tpu as pltpu

D_MODEL = 2048
SEQ = 2048
DEPTH = 2
A_HEADS = 8
A_HEAD_DIM = 64
B_HEADS = 8
B_HEAD_DIM = 128
B_GROUPS = ((128, 1), (512, 4), (2048, 16))
WIDTH = 1024
C_CONV = 3
N_BRANCH = 3
Q_BLOCK = 128
N_ALIBI = A_HEADS + len(B_GROUPS) * B_HEADS
RMS_EPS = 1e-6
N_IN = 4 * WIDTH + 9 * WIDTH + WIDTH + 4 * WIDTH + N_BRANCH * D_MODEL

LANE = 128
N_SLABS = N_IN // LANE
SLABS = WIDTH // LANE
SLAB_AQ, SLAB_AK, SLAB_AV, SLAB_AZ = 0, SLABS, 2 * SLABS, 3 * SLABS
SLAB_B = 4 * SLABS
SLAB_BZ = SLAB_B + 9 * SLABS
SLAB_CB = SLAB_BZ + SLABS
SLAB_CC, SLAB_CX, SLAB_CZ = SLAB_CB + SLABS, SLAB_CB + 2 * SLABS, SLAB_CB + 3 * SLABS
SLAB_GATE = SLAB_CB + 4 * SLABS
GATE_SLABS = N_BRANCH * D_MODEL // LANE

NEG_BIG = -1e30
LOG2E = 1.4426950408889634
VMEM_LIMIT = 56 * 1024 * 1024

_BF16 = jnp.bfloat16
_F32 = jnp.float32


def _silu(z):
    return z * jax.nn.sigmoid(z)


def _rmsnorm_kernel(x_ref, g_ref, o_ref):
    x = x_ref[...]
    ms = jnp.mean(x * x, axis=-1, keepdims=True)
    o_ref[...] = (x * lax.rsqrt(ms + RMS_EPS) * g_ref[...]).astype(o_ref.dtype)


def _rmsnorm(x2d, g, *, tm=1024):
    tokens = x2d.shape[0]
    return pl.pallas_call(
        _rmsnorm_kernel,
        grid=(tokens // tm,),
        in_specs=[pl.BlockSpec((tm, D_MODEL), lambda i: (i, 0)),
                  pl.BlockSpec((1, D_MODEL), lambda i: (0, 0))],
        out_specs=pl.BlockSpec((tm, D_MODEL), lambda i: (i, 0)),
        out_shape=jax.ShapeDtypeStruct((tokens, D_MODEL), _BF16),
        compiler_params=pltpu.CompilerParams(
            dimension_semantics=("arbitrary",), vmem_limit_bytes=VMEM_LIMIT),
        name="rmsnorm",
    )(x2d, g.reshape(1, D_MODEL))


def _inproj_kernel(h_ref, w_ref, o_ref, wb_ref):
    @pl.when(pl.program_id(1) == 0)
    def _():
        wb_ref[...] = w_ref[...].astype(_BF16)

    acc = jnp.dot(h_ref[...], wb_ref[...], preferred_element_type=_F32)
    for c in range(o_ref.shape[0]):
        o_ref[c] = acc[:, c * LANE:(c + 1) * LANE].astype(o_ref.dtype)


def _inproj(h, w_in, layer, *, tm=1024, tn=1024):
    tokens = h.shape[0]
    return pl.pallas_call(
        _inproj_kernel,
        grid=(N_IN // tn, tokens // tm),
        in_specs=[
            pl.BlockSpec((tm, D_MODEL), lambda j, i: (i, 0)),
            pl.BlockSpec((None, D_MODEL, tn), lambda j, i: (layer, 0, j)),
        ],
        out_specs=pl.BlockSpec((tn // LANE, tm, LANE), lambda j, i: (j, i, 0)),
        out_shape=jax.ShapeDtypeStruct((N_SLABS, tokens, LANE), _BF16),
        scratch_shapes=[pltpu.VMEM((D_MODEL, tn), _BF16)],
        compiler_params=pltpu.CompilerParams(
            dimension_semantics=("arbitrary", "arbitrary"), vmem_limit_bytes=VMEM_LIMIT),
        name="inproj",
    )(h, w_in)


ALIBI_PIECES = 3


def _attn_a_kernel(lamv_ref, q_ref, k_ref, v_ref, z_ref, bias_ref, sg_ref, o_ref, kk_ref, va_ref,
                   *, lam_init, tq, heads):
    seq = q_ref.shape[1]
    tk = tq
    lane_s = lax.broadcasted_iota(jnp.int32, (seq, LANE), 1)
    ones_col = jnp.where(lane_s == 0, 1.0, 0.0).astype(_BF16)
    for g in range(heads):
        kk_ref[g, :, :LANE] = k_ref[g]
        kk_ref[g, :, LANE:] = bias_ref[g]
        va_ref[g, :, :LANE] = v_ref[g]
        va_ref[g, :, LANE:] = ones_col

    lane = lax.broadcasted_iota(jnp.int32, (tq, LANE), 1)
    lane2 = lax.broadcasted_iota(jnp.int32, (2 * tq, LANE), 1)
    q_ones = jnp.where(lane2 < ALIBI_PIECES, 1.0, 0.0).astype(_BF16)
    row = lax.broadcasted_iota(jnp.int32, (2 * tq, tk), 0)
    causal = lax.broadcasted_iota(jnp.int32, (2 * tq, tk), 1) <= jnp.where(row >= tq, row - tq, row)
    lv = lamv_ref[...]
    lam = (jnp.exp(jnp.sum(lv[0:1] * lv[1:2], axis=-1, keepdims=True))
           - jnp.exp(jnp.sum(lv[2:3] * lv[3:4], axis=-1, keepdims=True)) + lam_init)

    def stacked_q(g, i):
        qf = q_ref[g, pl.ds(i * tq, tq), :].astype(_F32) * (A_HEAD_DIM ** -0.5)
        qq = jnp.concatenate([jnp.where(lane < A_HEAD_DIM, qf, 0.0),
                              jnp.where(lane >= A_HEAD_DIM, qf, 0.0)], axis=0).astype(_BF16)
        return jnp.concatenate([qq, q_ones], axis=1)

    def scores(qqs, i, j):
        out = []
        for g in range(heads):
            s = lax.dot_general(qqs[g], kk_ref[g, pl.ds(j * tk, tk), :], (((1,), (1,)), ((), ())),
                                preferred_element_type=_F32)
            out.append(jnp.where(causal, s, NEG_BIG) if j == i else s)
        return out

    def finish(i, carries):
        for g in range(heads):
            acc = carries[g][1]
            o = acc[:, :LANE] / acc[:, LANE:LANE + 1]
            out = o[:tq] - lam * o[tq:]
            ms = jnp.mean(out * out, axis=-1, keepdims=True)
            oa = out * lax.rsqrt(ms + RMS_EPS) * sg_ref[...] * (1.0 - lam_init)
            rows = pl.ds(i * tq, tq)
            o_ref[g, rows, :] = (oa * _silu(z_ref[g, rows, :].astype(_F32))).astype(o_ref.dtype)

    steps = [(i, j) for i in range(seq // tq) for j in range(i + 1)]
    init = (jnp.full((2 * tq, 1), NEG_BIG, _F32), jnp.zeros((2 * tq, 2 * LANE), _F32))
    qqs = [stacked_q(g, 0) for g in range(heads)]
    s_next = scores(qqs, 0, 0)
    carries = [init] * heads
    for t, (i, j) in enumerate(steps):
        s_cur = s_next
        if t + 1 < len(steps):
            ni, nj = steps[t + 1]
            if nj == 0:
                qqs = [stacked_q(g, ni) for g in range(heads)]
            s_next = scores(qqs, ni, nj)
        probs, m_news = [], []
        for g in range(heads):
            m_new = jnp.maximum(carries[g][0], jnp.max(s_cur[g], axis=-1, keepdims=True))
            probs.append(jnp.exp((s_cur[g] - m_new).astype(_BF16)))
            m_news.append(m_new)
        new = []
        for g in range(heads):
            m, acc = carries[g]
            pv = jnp.dot(probs[g], va_ref[g, pl.ds(j * tk, tk), :], preferred_element_type=_F32)
            new.append((m_news[g], jnp.exp(m - m_news[g]) * acc + pv))
        carries = new
        if j == i:
            finish(i, carries)
            carries = [init] * heads


def _alibi_table(slopes, seq):
    rest = slopes[:A_HEADS, None] * jnp.arange(seq, dtype=_F32)[None, :]
    pieces = []
    for _ in range(ALIBI_PIECES):
        bits = lax.bitcast_convert_type(rest, jnp.uint32) & jnp.uint32(0xFFFF0000)
        part = lax.bitcast_convert_type(bits, _F32)
        pieces.append(part.astype(_BF16))
        rest = rest - part
    tab = jnp.stack(pieces, axis=-1)
    return jnp.pad(tab, ((0, 0), (0, 0), (0, LANE - ALIBI_PIECES)))


def _attn_a(proj4, slopes, lamv, subln_g, *, lam_init, tq=512, heads=2):
    bsz, seq = proj4.shape[1], proj4.shape[2]

    def slabs(first):
        return pl.BlockSpec((heads, None, seq, LANE), lambda b, h: (first // heads + h, b, 0, 0))

    return pl.pallas_call(
        functools.partial(_attn_a_kernel, lam_init=lam_init, tq=tq, heads=heads),
        grid=(bsz, A_HEADS // heads),
        in_specs=[
            pl.BlockSpec((4, A_HEAD_DIM), lambda b, h: (0, 0)),
            slabs(SLAB_AQ), slabs(SLAB_AK), slabs(SLAB_AV), slabs(SLAB_AZ),
            pl.BlockSpec((heads, seq, LANE), lambda b, h: (h, 0, 0)),
            pl.BlockSpec((1, LANE), lambda b, h: (0, 0)),
        ],
        out_specs=slabs(0),
        out_shape=jax.ShapeDtypeStruct((SLABS, bsz, seq, LANE), _BF16),
        scratch_shapes=[pltpu.VMEM((heads, seq, 2 * LANE), _BF16)] * 2,
        compiler_params=pltpu.CompilerParams(
            dimension_semantics=("arbitrary", "arbitrary"), vmem_limit_bytes=VMEM_LIMIT),
        name="attn_a",
    )(lamv, proj4, proj4, proj4, proj4, _alibi_table(slopes, seq), subln_g.reshape(1, LANE))


def _band_bias(slope, dil):
    qi = lax.broadcasted_iota(jnp.int32, (Q_BLOCK, 2 * Q_BLOCK), 0)
    ki = lax.broadcasted_iota(jnp.int32, (Q_BLOCK, 2 * Q_BLOCK), 1)
    off = Q_BLOCK + qi - ki
    valid = (off >= 0) & (off <= Q_BLOCK)
    return jnp.where(valid, -(slope * dil) * off.astype(_F32), NEG_BIG)


BAND_LOOKAHEAD = 2


def _attn_b_kernel(slopes_ref, q1_ref, k1_ref, v1_ref, q2_ref, k2_ref, v2_ref,
                   q3_ref, k3_ref, v3_ref, z_ref, o_ref, f_ref, og_ref, lg_ref):
    seq = o_ref.shape[0]
    h = pl.program_id(1)
    group_refs = ((q1_ref, k1_ref, v1_ref), (q2_ref, k2_ref, v2_ref), (q3_ref, k3_ref, v3_ref))
    ones_blk = jnp.ones((2 * Q_BLOCK, LANE), _BF16)

    biases, blocks = [], []
    for g, (window, dil) in enumerate(B_GROUPS):
        assert window // dil == Q_BLOCK
        biases.append(_band_bias(slopes_ref[A_HEADS + g * B_HEADS + h], dil))
        qr, kr, vr = group_refs[g]
        if dil > 1:
            f_ref[g - 1, 0] = qr[...].astype(_F32)
            f_ref[g - 1, 1] = kr[...].astype(_F32)
            f_ref[g - 1, 2] = vr[...].astype(_F32)
        for r in range(dil):
            for n in range(seq // dil // Q_BLOCK):
                blocks.append((g, dil, r, n))

    def rows_of(dil, r, n):
        first = n == 0
        k0 = n * Q_BLOCK if first else (n - 1) * Q_BLOCK
        nk = Q_BLOCK if first else 2 * Q_BLOCK
        if dil == 1:
            return pl.ds(n * Q_BLOCK, Q_BLOCK), pl.ds(k0, nk), nk
        return (pl.ds(n * Q_BLOCK * dil + r, Q_BLOCK, stride=dil),
                pl.ds(k0 * dil + r, nk, stride=dil), nk)

    def load(g, part, rows):
        if g == 0:
            return group_refs[0][part][rows, :]
        return f_ref[g - 1, part, rows, :].astype(_BF16)

    def block_scores(g, dil, r, n):
        q_rows, k_rows, nk = rows_of(dil, r, n)
        s = lax.dot_general(load(g, 0, q_rows), load(g, 1, k_rows), (((1,), (1,)), ((), ())),
                            preferred_element_type=_F32)
        return s * (B_HEAD_DIM ** -0.5) + biases[g][:, 2 * Q_BLOCK - nk:]

    def block_finish(s, g, dil, r, n):
        q_rows, k_rows, nk = rows_of(dil, r, n)
        m = jnp.max(s, axis=-1, keepdims=True)
        p = jnp.exp((s - m).astype(_BF16))
        v_ones = jnp.concatenate([load(g, 2, k_rows), ones_blk[:nk]], axis=1)
        pv = jnp.dot(p, v_ones, preferred_element_type=_F32)
        l = pv[:, LANE:]
        og_ref[g, q_rows, :] = pv[:, :LANE] / l
        lg_ref[g, q_rows, :] = m + jnp.log(l)

    pending = []
    for t in range(len(blocks) + BAND_LOOKAHEAD):
        if t < len(blocks):
            pending.append(block_scores(*blocks[t]))
        if t >= BAND_LOOKAHEAD:
            block_finish(pending.pop(0), *blocks[t - BAND_LOOKAHEAD])

    rows = 256
    for c in range(seq // rows):
        sl = pl.ds(c * rows, rows)
        l0, l1, l2 = lg_ref[0, sl, :], lg_ref[1, sl, :], lg_ref[2, sl, :]
        mx = jnp.maximum(jnp.maximum(l0, l1), l2)
        w0, w1, w2 = jnp.exp(l0 - mx), jnp.exp(l1 - mx), jnp.exp(l2 - mx)
        ob = (w0 * og_ref[0, sl, :] + w1 * og_ref[1, sl, :] + w2 * og_ref[2, sl, :]) / (w0 + w1 + w2)
        o_ref[sl, :] = (ob * _silu(z_ref[sl, :].astype(_F32))).astype(o_ref.dtype)


def _attn_b(proj4, slopes):
    bsz, seq = proj4.shape[1], proj4.shape[2]

    def slab(first):
        return pl.BlockSpec((None, None, seq, LANE), lambda b, h: (first + h, b, 0, 0))

    in_specs = [pl.BlockSpec(memory_space=pltpu.SMEM)]
    for g in range(len(B_GROUPS)):
        for part in range(3):
            in_specs.append(slab(SLAB_B + (3 * g + part) * SLABS))
    in_specs.append(slab(SLAB_BZ))
    return pl.pallas_call(
        _attn_b_kernel,
        grid=(bsz, B_HEADS),
        in_specs=in_specs,
        out_specs=pl.BlockSpec((None, None, seq, LANE), lambda b, h: (h, b, 0, 0)),
        out_shape=jax.ShapeDtypeStruct((SLABS, bsz, seq, LANE), _BF16),
        scratch_shapes=[pltpu.VMEM((2, 3, seq, LANE), _F32), pltpu.VMEM((3, seq, LANE), _F32),
                        pltpu.VMEM((3, seq, LANE), _F32)],
        compiler_params=pltpu.CompilerParams(
            dimension_semantics=("arbitrary", "arbitrary"), vmem_limit_bytes=VMEM_LIMIT),
        name="attn_b",
    )(slopes, *([proj4] * 10))


def _conv_kernel(cb_ref, cc_ref, cx_ref, cz_ref, w_ref, o_ref):
    u = cc_ref[...].astype(_F32) * cx_ref[...].astype(_F32)
    row = lax.broadcasted_iota(jnp.int32, u.shape, 0)
    u1 = jnp.where(row >= 1, pltpu.roll(u, 1, 0), 0.0)
    u2 = jnp.where(row >= 2, pltpu.roll(u, 2, 0), 0.0)
    w = w_ref[...]
    conv = w[0:1] * u2 + w[1:2] * u1 + w[2:3] * u
    o_ref[...] = (cb_ref[...].astype(_F32) * conv * _silu(cz_ref[...].astype(_F32))).astype(o_ref.dtype)


def _conv(proj4, conv_w):
    bsz, seq = proj4.shape[1], proj4.shape[2]

    def slab(first):
        return pl.BlockSpec((None, None, seq, LANE), lambda b, c: (first + c, b, 0, 0))

    return pl.pallas_call(
        _conv_kernel,
        grid=(bsz, SLABS),
        in_specs=[slab(SLAB_CB), slab(SLAB_CC), slab(SLAB_CX), slab(SLAB_CZ),
                  pl.BlockSpec((C_CONV, LANE), lambda b, c: (0, c))],
        out_specs=pl.BlockSpec((None, None, seq, LANE), lambda b, c: (c, b, 0, 0)),
        out_shape=jax.ShapeDtypeStruct((SLABS, bsz, seq, LANE), _BF16),
        compiler_params=pltpu.CompilerParams(
            dimension_semantics=("arbitrary", "arbitrary"), vmem_limit_bytes=VMEM_LIMIT),
        name="conv",
    )(proj4, proj4, proj4, proj4, conv_w)


def _merge_kernel(ya_ref, yb_ref, yc_ref, gp_ref, gb_ref, x_ref, wa_ref, wb_ref, wc_ref, wo_ref,
                  ng_ref, *rest, final):
    if final:
        o_ref, cat_ref, mrg_ref = rest
    else:
        o_ref, h_ref, cat_ref, mrg_ref = rest
    branch = []
    for y_ref, w_ref in ((ya_ref, wa_ref), (yb_ref, wb_ref), (yc_ref, wc_ref)):
        for c in range(SLABS):
            cat_ref[:, c * LANE:(c + 1) * LANE] = y_ref[c]
        branch.append(jnp.dot(cat_ref[...], w_ref[...], preferred_element_type=_F32))
    per_branch = D_MODEL // LANE
    for c in range(per_branch):
        cols = slice(c * LANE, (c + 1) * LANE)
        tot = None
        for n in range(N_BRANCH):
            gcols = slice(n * D_MODEL + c * LANE, n * D_MODEL + (c + 1) * LANE)
            gate = jax.nn.sigmoid(gp_ref[n * per_branch + c].astype(_F32) + gb_ref[:, gcols])
            term = gate * branch[n][:, cols]
            tot = term if tot is None else tot + term
        mrg_ref[:, cols] = tot.astype(_BF16)
    out = x_ref[...] + jnp.dot(mrg_ref[...], wo_ref[...], preferred_element_type=_F32)
    ms = jnp.mean(out * out, axis=-1, keepdims=True)
    normed = out * lax.rsqrt(ms + RMS_EPS) * ng_ref[...]
    if final:
        o_ref[...] = normed
    else:
        o_ref[...] = out
        h_ref[...] = normed.astype(h_ref.dtype)


def _merge(ya, yb, yc, proj, gate_b, x2d, wa, wb, wc, wo, next_g, *, final, tm=256):
    tokens = x2d.shape[0]
    row_spec = pl.BlockSpec((tm, D_MODEL), lambda i: (i, 0))
    if final:
        out_specs, out_shape = row_spec, jax.ShapeDtypeStruct((tokens, D_MODEL), _F32)
    else:
        out_specs = (row_spec, row_spec)
        out_shape = (jax.ShapeDtypeStruct((tokens, D_MODEL), _F32),
                     jax.ShapeDtypeStruct((tokens, D_MODEL), _BF16))
    const = lambda shape: pl.BlockSpec(shape, lambda i: (0,) * len(shape),
                                       pipeline_mode=pl.Buffered(1))
    yspec = pl.BlockSpec((SLABS, tm, LANE), lambda i: (0, i, 0))
    return pl.pallas_call(
        functools.partial(_merge_kernel, final=final),
        grid=(tokens // tm,),
        in_specs=[
            yspec, yspec, yspec,
            pl.BlockSpec((GATE_SLABS, tm, LANE), lambda i: (SLAB_GATE // GATE_SLABS, i, 0)),
            const((1, N_BRANCH * D_MODEL)),
            pl.BlockSpec((tm, D_MODEL), lambda i: (i, 0)),
            const((WIDTH, D_MODEL)), const((WIDTH, D_MODEL)), const((WIDTH, D_MODEL)),
            const((D_MODEL, D_MODEL)),
            const((1, D_MODEL)),
        ],
        out_specs=out_specs,
        out_shape=out_shape,
        scratch_shapes=[pltpu.VMEM((tm, WIDTH), _BF16), pltpu.VMEM((tm, D_MODEL), _BF16)],
        compiler_params=pltpu.CompilerParams(
            dimension_semantics=("arbitrary",), vmem_limit_bytes=VMEM_LIMIT),
        name="merge",
    )(ya, yb, yc, proj, gate_b.reshape(1, N_BRANCH * D_MODEL), x2d, wa, wb, wc, wo,
      next_g.reshape(1, D_MODEL))


def kernel(x, norm_g, w_in, gate_b, lambda_q1, lambda_k1, lambda_q2, lambda_k2, subln_g, conv_w,
           w_branch_a, w_branch_b, w_branch_c, w_out, final_g):
    bsz, seq, _ = x.shape
    assert seq == SEQ and SLAB_GATE % GATE_SLABS == 0
    tokens = bsz * seq
    slopes = jnp.asarray(2.0 ** (-8.0 * np.arange(1, N_ALIBI + 1) / N_ALIBI), dtype=_F32)
    x2d = x.reshape(tokens, D_MODEL)
    h = _rmsnorm(x2d, norm_g[0])
    for l in range(DEPTH):
        final = l == DEPTH - 1
        lam_init = 0.8 - 0.6 * math.exp(-0.3 * l)
        lamv = jnp.stack([lambda_q1[l], lambda_k1[l], lambda_q2[l], lambda_k2[l]])
        proj = _inproj(h, w_in, l)
        proj4 = proj.reshape(N_SLABS, bsz, seq, LANE)
        ya = _attn_a(proj4, slopes, lamv, subln_g[l], lam_init=lam_init)
        yb = _attn_b(proj4, slopes)
        yc = _conv(proj4, conv_w[l])
        to3 = lambda y: y.reshape(SLABS, tokens, LANE)
        res = _merge(to3(ya), to3(yb), to3(yc), proj, gate_b[l], x2d,
                     w_branch_a[l].astype(_BF16), w_branch_b[l].astype(_BF16),
                     w_branch_c[l].astype(_BF16), w_out[l].astype(_BF16),
                     final_g if final else norm_g[l + 1], final=final)
        x2d, h = (res, None) if final else res
    return x2d.reshape(bsz, seq, D_MODEL)
```

```python
import functools
import math

import jax
import jax.numpy as jnp
import numpy as np
from jax import lax
from jax.experimental import pallas as pl
from jax.experimental.pallas import tpu as pltpu

D_MODEL = 2048
SEQ = 2048
DEPTH = 2
A_HEADS = 8
A_HEAD_DIM = 64
B_HEADS = 8
B_HEAD_DIM = 128
B_GROUPS = ((128, 1), (512, 4), (2048, 16))
WIDTH = 1024
C_CONV = 3
N_BRANCH = 3
Q_BLOCK = 128
N_ALIBI = A_HEADS + len(B_GROUPS) * B_HEADS
RMS_EPS = 1e-6
N_IN = 4 * WIDTH + 9 * WIDTH + WIDTH + 4 * WIDTH + N_BRANCH * D_MODEL

LANE = 128
N_SLABS = N_IN // LANE
SLABS = WIDTH // LANE
SLAB_AQ, SLAB_AK, SLAB_AV, SLAB_AZ = 0, SLABS, 2 * SLABS, 3 * SLABS
SLAB_B = 4 * SLABS
SLAB_BZ = SLAB_B + 9 * SLABS
SLAB_CB = SLAB_BZ + SLABS
SLAB_CC, SLAB_CX, SLAB_CZ = SLAB_CB + SLABS, SLAB_CB + 2 * SLABS, SLAB_CB + 3 * SLABS
SLAB_GATE = SLAB_CB + 4 * SLABS
GATE_SLABS = N_BRANCH * D_MODEL // LANE

NEG_BIG = -1e30
LOG2E = 1.4426950408889634
VMEM_LIMIT = 56 * 1024 * 1024

_BF16 = jnp.bfloat16
_F32 = jnp.float32


def _silu(z):
    return z * jax.nn.sigmoid(z)


def _rmsnorm_kernel(x_ref, g_ref, o_ref):
    x = x_ref[...]
    ms = jnp.mean(x * x, axis=-1, keepdims=True)
    o_ref[...] = (x * lax.rsqrt(ms + RMS_EPS) * g_ref[...]).astype(o_ref.dtype)


def _rmsnorm(x2d, g, *, tm=1024):
    tokens = x2d.shape[0]
    return pl.pallas_call(
        _rmsnorm_kernel,
        grid=(tokens // tm,),
        in_specs=[pl.BlockSpec((tm, D_MODEL), lambda i: (i, 0)),
                  pl.BlockSpec((1, D_MODEL), lambda i: (0, 0))],
        out_specs=pl.BlockSpec((tm, D_MODEL), lambda i: (i, 0)),
        out_shape=jax.ShapeDtypeStruct((tokens, D_MODEL), _BF16),
        compiler_params=pltpu.CompilerParams(
            dimension_semantics=("arbitrary",), vmem_limit_bytes=VMEM_LIMIT),
        name="rmsnorm",
    )(x2d, g.reshape(1, D_MODEL))


def _inproj_kernel(h_ref, w_ref, o_ref, wb_ref):
    @pl.when(pl.program_id(1) == 0)
    def _():
        wb_ref[...] = w_ref[...].astype(_BF16)

    acc = jnp.dot(h_ref[...], wb_ref[...], preferred_element_type=_F32)
    for c in range(o_ref.shape[0]):
        o_ref[c] = acc[:, c * LANE:(c + 1) * LANE].astype(o_ref.dtype)


def _inproj(h, w_in, layer, *, tm=2048, tn=1024):
    tokens = h.shape[0]
    return pl.pallas_call(
        _inproj_kernel,
        grid=(N_IN // tn, tokens // tm),
        in_specs=[
            pl.BlockSpec((tm, D_MODEL), lambda j, i: (i, 0)),
            pl.BlockSpec((None, D_MODEL, tn), lambda j, i: (layer, 0, j)),
        ],
        out_specs=pl.BlockSpec((tn // LANE, tm, LANE), lambda j, i: (j, i, 0)),
        out_shape=jax.ShapeDtypeStruct((N_SLABS, tokens, LANE), _BF16),
        scratch_shapes=[pltpu.VMEM((D_MODEL, tn), _BF16)],
        compiler_params=pltpu.CompilerParams(
            dimension_semantics=("arbitrary", "arbitrary"), vmem_limit_bytes=VMEM_LIMIT),
        name="inproj",
    )(h, w_in)


ALIBI_PIECES = 3


def _attn_a_kernel(lamv_ref, q_ref, k_ref, v_ref, z_ref, bias_ref, sg_ref, o_ref, kk_ref, va_ref,
                   *, lam_init, tq, heads):
    seq = q_ref.shape[1]
    tk = tq
    lane_s = lax.broadcasted_iota(jnp.int32, (seq, LANE), 1)
    ones_col = jnp.where(lane_s == 0, 1.0, 0.0).astype(_BF16)
    for g in range(heads):
        kk_ref[g, :, :LANE] = k_ref[g]
        kk_ref[g, :, LANE:] = bias_ref[g]
        va_ref[g, :, :LANE] = v_ref[g]
        va_ref[g, :, LANE:] = ones_col

    lane = lax.broadcasted_iota(jnp.int32, (tq, LANE), 1)
    lane2 = lax.broadcasted_iota(jnp.int32, (2 * tq, LANE), 1)
    q_ones = jnp.where(lane2 < ALIBI_PIECES, 1.0, 0.0).astype(_BF16)
    row = lax.broadcasted_iota(jnp.int32, (2 * tq, tk), 0)
    causal = lax.broadcasted_iota(jnp.int32, (2 * tq, tk), 1) <= jnp.where(row >= tq, row - tq, row)
    lv = lamv_ref[...]
    lam = (jnp.exp(jnp.sum(lv[0:1] * lv[1:2], axis=-1, keepdims=True))
           - jnp.exp(jnp.sum(lv[2:3] * lv[3:4], axis=-1, keepdims=True)) + lam_init)

    def stacked_q(g, i):
        qf = q_ref[g, pl.ds(i * tq, tq), :].astype(_F32) * (A_HEAD_DIM ** -0.5)
        qq = jnp.concatenate([jnp.where(lane < A_HEAD_DIM, qf, 0.0),
                              jnp.where(lane >= A_HEAD_DIM, qf, 0.0)], axis=0).astype(_BF16)
        return jnp.concatenate([qq, q_ones], axis=1)

    def scores(qqs, i, j):
        out = []
        for g in range(heads):
            s = lax.dot_general(qqs[g], kk_ref[g, pl.ds(j * tk, tk), :], (((1,), (1,)), ((), ())),
                                preferred_element_type=_F32)
            out.append(jnp.where(causal, s, NEG_BIG) if j == i else s)
        return out

    def finish(i, carries):
        for g in range(heads):
            acc = carries[g][1]
            o = acc[:, :LANE] / acc[:, LANE:LANE + 1]
            out = o[:tq] - lam * o[tq:]
            ms = jnp.mean(out * out, axis=-1, keepdims=True)
            oa = out * lax.rsqrt(ms + RMS_EPS) * sg_ref[...] * (1.0 - lam_init)
            rows = pl.ds(i * tq, tq)
            o_ref[g, rows, :] = (oa * _silu(z_ref[g, rows, :].astype(_F32))).astype(o_ref.dtype)

    steps = [(i, j) for i in range(seq // tq) for j in range(i + 1)]
    init = (jnp.full((2 * tq, 1), NEG_BIG, _F32), jnp.zeros((2 * tq, 2 * LANE), _F32))
    qqs = [stacked_q(g, 0) for g in range(heads)]
    s_next = scores(qqs, 0, 0)
    carries = [init] * heads
    for t, (i, j) in enumerate(steps):
        s_cur = s_next
        if t + 1 < len(steps):
            ni, nj = steps[t + 1]
            if nj == 0:
                qqs = [stacked_q(g, ni) for g in range(heads)]
            s_next = scores(qqs, ni, nj)
        probs, m_news = [], []
        for g in range(heads):
            m_new = jnp.maximum(carries[g][0], jnp.max(s_cur[g], axis=-1, keepdims=True))
            probs.append(jnp.exp((s_cur[g] - m_new).astype(_BF16)))
            m_news.append(m_new)
        new = []
        for g in range(heads):
            m, acc = carries[g]
            pv = jnp.dot(probs[g], va_ref[g, pl.ds(j * tk, tk), :], preferred_element_type=_F32)
            new.append((m_news[g], jnp.exp(m - m_news[g]) * acc + pv))
        carries = new
        if j == i:
            finish(i, carries)
            carries = [init] * heads


def _alibi_table(slopes, seq):
    rest = slopes[:A_HEADS, None] * jnp.arange(seq, dtype=_F32)[None, :]
    pieces = []
    for _ in range(ALIBI_PIECES):
        bits = lax.bitcast_convert_type(rest, jnp.uint32) & jnp.uint32(0xFFFF0000)
        part = lax.bitcast_convert_type(bits, _F32)
        pieces.append(part.astype(_BF16))
        rest = rest - part
    tab = jnp.stack(pieces, axis=-1)
    return jnp.pad(tab, ((0, 0), (0, 0), (0, LANE - ALIBI_PIECES)))


def _attn_a(proj4, slopes, lamv, subln_g, *, lam_init, tq=512, heads=2):
    bsz, seq = proj4.shape[1], proj4.shape[2]

    def slabs(first):
        return pl.BlockSpec((heads, None, seq, LANE), lambda b, h: (first // heads + h, b, 0, 0))

    return pl.pallas_call(
        functools.partial(_attn_a_kernel, lam_init=lam_init, tq=tq, heads=heads),
        grid=(bsz, A_HEADS // heads),
        in_specs=[
            pl.BlockSpec((4, A_HEAD_DIM), lambda b, h: (0, 0)),
            slabs(SLAB_AQ), slabs(SLAB_AK), slabs(SLAB_AV), slabs(SLAB_AZ),
            pl.BlockSpec((heads, seq, LANE), lambda b, h: (h, 0, 0)),
            pl.BlockSpec((1, LANE), lambda b, h: (0, 0)),
        ],
        out_specs=slabs(0),
        out_shape=jax.ShapeDtypeStruct((SLABS, bsz, seq, LANE), _BF16),
        scratch_shapes=[pltpu.VMEM((heads, seq, 2 * LANE), _BF16)] * 2,
        compiler_params=pltpu.CompilerParams(
            dimension_semantics=("arbitrary", "arbitrary"), vmem_limit_bytes=VMEM_LIMIT),
        name="attn_a",
    )(lamv, proj4, proj4, proj4, proj4, _alibi_table(slopes, seq), subln_g.reshape(1, LANE))


def _band_bias(slope, dil):
    qi = lax.broadcasted_iota(jnp.int32, (Q_BLOCK, 2 * Q_BLOCK), 0)
    ki = lax.broadcasted_iota(jnp.int32, (Q_BLOCK, 2 * Q_BLOCK), 1)
    off = Q_BLOCK + qi - ki
    valid = (off >= 0) & (off <= Q_BLOCK)
    return jnp.where(valid, -(slope * dil) * off.astype(_F32), NEG_BIG)


BAND_LOOKAHEAD = 2


def _attn_b_kernel(slopes_ref, q1_ref, k1_ref, v1_ref, q2_ref, k2_ref, v2_ref,
                   q3_ref, k3_ref, v3_ref, z_ref, o_ref, f_ref, og_ref, lg_ref):
    seq = o_ref.shape[0]
    h = pl.program_id(1)
    group_refs = ((q1_ref, k1_ref, v1_ref), (q2_ref, k2_ref, v2_ref), (q3_ref, k3_ref, v3_ref))
    ones_blk = jnp.ones((2 * Q_BLOCK, LANE), _BF16)

    biases, blocks = [], []
    for g, (window, dil) in enumerate(B_GROUPS):
        assert window // dil == Q_BLOCK
        biases.append(_band_bias(slopes_ref[A_HEADS + g * B_HEADS + h], dil))
        qr, kr, vr = group_refs[g]
        if dil > 1:
            f_ref[g - 1, 0] = qr[...].astype(_F32)
            f_ref[g - 1, 1] = kr[...].astype(_F32)
            f_ref[g - 1, 2] = vr[...].astype(_F32)
        for r in range(dil):
            for n in range(seq // dil // Q_BLOCK):
                blocks.append((g, dil, r, n))

    def rows_of(dil, r, n):
        first = n == 0
        k0 = n * Q_BLOCK if first else (n - 1) * Q_BLOCK
        nk = Q_BLOCK if first else 2 * Q_BLOCK
        if dil == 1:
            return pl.ds(n * Q_BLOCK, Q_BLOCK), pl.ds(k0, nk), nk
        return (pl.ds(n * Q_BLOCK * dil + r, Q_BLOCK, stride=dil),
                pl.ds(k0 * dil + r, nk, stride=dil), nk)

    def load(g, part, rows):
        if g == 0:
            return group_refs[0][part][rows, :]
        return f_ref[g - 1, part, rows, :].astype(_BF16)

    def block_scores(g, dil, r, n):
        q_rows, k_rows, nk = rows_of(dil, r, n)
        s = lax.dot_general(load(g, 0, q_rows), load(g, 1, k_rows), (((1,), (1,)), ((), ())),
                            preferred_element_type=_F32)
        return s * (B_HEAD_DIM ** -0.5) + biases[g][:, 2 * Q_BLOCK - nk:]

    def block_finish(s, g, dil, r, n):
        q_rows, k_rows, nk = rows_of(dil, r, n)
        m = jnp.max(s, axis=-1, keepdims=True)
        p = jnp.exp((s - m).astype(_BF16))
        v_ones = jnp.concatenate([load(g, 2, k_rows), ones_blk[:nk]], axis=1)
        pv = jnp.dot(p, v_ones, preferred_element_type=_F32)
        l = pv[:, LANE:]
        og_ref[g, q_rows, :] = pv[:, :LANE] / l
        lg_ref[g, q_rows, :] = m + jnp.log(l)

    pending = []
    for t in range(len(blocks) + BAND_LOOKAHEAD):
        if t < len(blocks):
            pending.append(block_scores(*blocks[t]))
        if t >= BAND_LOOKAHEAD:
            block_finish(pending.pop(0), *blocks[t - BAND_LOOKAHEAD])

    rows = 256
    for c in range(seq // rows):
        sl = pl.ds(c * rows, rows)
        l0, l1, l2 = lg_ref[0, sl, :], lg_ref[1, sl, :], lg_ref[2, sl, :]
        mx = jnp.maximum(jnp.maximum(l0, l1), l2)
        w0, w1, w2 = jnp.exp(l0 - mx), jnp.exp(l1 - mx), jnp.exp(l2 - mx)
        ob = (w0 * og_ref[0, sl, :] + w1 * og_ref[1, sl, :] + w2 * og_ref[2, sl, :]) / (w0 + w1 + w2)
        o_ref[sl, :] = (ob * _silu(z_ref[sl, :].astype(_F32))).astype(o_ref.dtype)


def _attn_b(proj4, slopes):
    bsz, seq = proj4.shape[1], proj4.shape[2]

    def slab(first):
        return pl.BlockSpec((None, None, seq, LANE), lambda b, h: (first + h, b, 0, 0))

    in_specs = [pl.BlockSpec(memory_space=pltpu.SMEM)]
    for g in range(len(B_GROUPS)):
        for part in range(3):
            in_specs.append(slab(SLAB_B + (3 * g + part) * SLABS))
    in_specs.append(slab(SLAB_BZ))
    return pl.pallas_call(
        _attn_b_kernel,
        grid=(bsz, B_HEADS),
        in_specs=in_specs,
        out_specs=pl.BlockSpec((None, None, seq, LANE), lambda b, h: (h, b, 0, 0)),
        out_shape=jax.ShapeDtypeStruct((SLABS, bsz, seq, LANE), _BF16),
        scratch_shapes=[pltpu.VMEM((2, 3, seq, LANE), _F32), pltpu.VMEM((3, seq, LANE), _F32),
                        pltpu.VMEM((3, seq, LANE), _F32)],
        compiler_params=pltpu.CompilerParams(
            dimension_semantics=("arbitrary", "arbitrary"), vmem_limit_bytes=VMEM_LIMIT),
        name="attn_b",
    )(slopes, *([proj4] * 10))


CONV_HALO = 8


def _merge_kernel(ya_ref, yb_ref, cbc_ref, cxz_ref, cw_ref, gp_ref, gb_ref, x_ref, wa_ref, wb_ref,
                  wc_ref, wo_ref, ng_ref, *rest, final, tiles_per_seq):
    if final:
        o_ref, cat_ref, mrg_ref, halo_ref = rest
    else:
        o_ref, h_ref, cat_ref, mrg_ref, halo_ref = rest
    tm = x_ref.shape[0]
    branch = []
    for y_ref, w_ref in ((ya_ref, wa_ref), (yb_ref, wb_ref)):
        for c in range(SLABS):
            cat_ref[:, c * LANE:(c + 1) * LANE] = y_ref[c]
        branch.append(jnp.dot(cat_ref[...], w_ref[...], preferred_element_type=_F32))

    seq_start = pl.program_id(0) % tiles_per_seq == 0
    for c in range(SLABS):
        cols = slice(c * LANE, (c + 1) * LANE)
        u = cbc_ref[SLABS + c].astype(_F32) * cxz_ref[c].astype(_F32)
        ext = jnp.concatenate([jnp.where(seq_start, 0.0, halo_ref[c]), u], axis=0)
        halo_ref[c] = u[tm - CONV_HALO:]
        u1 = pltpu.roll(ext, 1, 0)[CONV_HALO:]
        u2 = pltpu.roll(ext, 2, 0)[CONV_HALO:]
        conv = cw_ref[0:1, cols] * u2 + cw_ref[1:2, cols] * u1 + cw_ref[2:3, cols] * u
        yc = cbc_ref[c].astype(_F32) * conv * _silu(cxz_ref[SLABS + c].astype(_F32))
        cat_ref[:, cols] = yc.astype(_BF16)
    branch.append(jnp.dot(cat_ref[...], wc_ref[...], preferred_element_type=_F32))
    per_branch = D_MODEL // LANE
    for c in range(per_branch):
        cols = slice(c * LANE, (c + 1) * LANE)
        tot = None
        for n in range(N_BRANCH):
            gcols = slice(n * D_MODEL + c * LANE, n * D_MODEL + (c + 1) * LANE)
            gate = jax.nn.sigmoid(gp_ref[n * per_branch + c].astype(_F32) + gb_ref[:, gcols])
            term = gate * branch[n][:, cols]
            tot = term if tot is None else tot + term
        mrg_ref[:, cols] = tot.astype(_BF16)
    out = x_ref[...] + jnp.dot(mrg_ref[...], wo_ref[...], preferred_element_type=_F32)
    ms = jnp.mean(out * out, axis=-1, keepdims=True)
    normed = out * lax.rsqrt(ms + RMS_EPS) * ng_ref[...]
    if final:
        o_ref[...] = normed
    else:
        o_ref[...] = out
        h_ref[...] = normed.astype(h_ref.dtype)


def _merge(ya, yb, proj, conv_w, gate_b, x2d, wa, wb, wc, wo, next_g, *, final, seq, tm=256):
    tokens = x2d.shape[0]
    assert seq % tm == 0 and SLAB_CB % (2 * SLABS) == 0 and SLAB_CX % (2 * SLABS) == 0
    row_spec = pl.BlockSpec((tm, D_MODEL), lambda i: (i, 0))
    if final:
        out_specs, out_shape = row_spec, jax.ShapeDtypeStruct((tokens, D_MODEL), _F32)
    else:
        out_specs = (row_spec, row_spec)
        out_shape = (jax.ShapeDtypeStruct((tokens, D_MODEL), _F32),
                     jax.ShapeDtypeStruct((tokens, D_MODEL), _BF16))
    const = lambda shape: pl.BlockSpec(shape, lambda i: (0,) * len(shape),
                                       pipeline_mode=pl.Buffered(1))
    yspec = pl.BlockSpec((SLABS, tm, LANE), lambda i: (0, i, 0))
    return pl.pallas_call(
        functools.partial(_merge_kernel, final=final, tiles_per_seq=seq // tm),
        grid=(tokens // tm,),
        in_specs=[
            yspec, yspec,
            pl.BlockSpec((2 * SLABS, tm, LANE), lambda i: (SLAB_CB // (2 * SLABS), i, 0)),
            pl.BlockSpec((2 * SLABS, tm, LANE), lambda i: (SLAB_CX // (2 * SLABS), i, 0)),
            const((C_CONV, WIDTH)),
            pl.BlockSpec((GATE_SLABS, tm, LANE), lambda i: (SLAB_GATE // GATE_SLABS, i, 0)),
            const((1, N_BRANCH * D_MODEL)),
            pl.BlockSpec((tm, D_MODEL), lambda i: (i, 0)),
            const((WIDTH, D_MODEL)), const((WIDTH, D_MODEL)), const((WIDTH, D_MODEL)),
            const((D_MODEL, D_MODEL)),
            const((1, D_MODEL)),
        ],
        out_specs=out_specs,
        out_shape=out_shape,
        scratch_shapes=[pltpu.VMEM((tm, WIDTH), _BF16), pltpu.VMEM((tm, D_MODEL), _BF16),
                        pltpu.VMEM((SLABS, CONV_HALO, LANE), _F32)],
        compiler_params=pltpu.CompilerParams(
            dimension_semantics=("arbitrary",), vmem_limit_bytes=VMEM_LIMIT),
        name="merge",
    )(ya, yb, proj, proj, conv_w, proj, gate_b.reshape(1, N_BRANCH * D_MODEL), x2d, wa, wb, wc, wo,
      next_g.reshape(1, D_MODEL))


def kernel(x, norm_g, w_in, gate_b, lambda_q1, lambda_k1, lambda_q2, lambda_k2, subln_g, conv_w,
           w_branch_a, w_branch_b, w_branch_c, w_out, final_g):
    bsz, seq, _ = x.shape
    assert seq == SEQ and SLAB_GATE % GATE_SLABS == 0
    tokens = bsz * seq
    slopes = jnp.asarray(2.0 ** (-8.0 * np.arange(1, N_ALIBI + 1) / N_ALIBI), dtype=_F32)
    x2d = x.reshape(tokens, D_MODEL)
    h = _rmsnorm(x2d, norm_g[0])
    for l in range(DEPTH):
        final = l == DEPTH - 1
        lam_init = 0.8 - 0.6 * math.exp(-0.3 * l)
        lamv = jnp.stack([lambda_q1[l], lambda_k1[l], lambda_q2[l], lambda_k2[l]])
        proj = _inproj(h, w_in, l)
        proj4 = proj.reshape(N_SLABS, bsz, seq, LANE)
        ya = _attn_a(proj4, slopes, lamv, subln_g[l], lam_init=lam_init)
        yb = _attn_b(proj4, slopes)
        to3 = lambda y: y.reshape(SLABS, tokens, LANE)
        res = _merge(to3(ya), to3(yb), proj, conv_w[l], gate_b[l], x2d,
                     w_branch_a[l].astype(_BF16), w_branch_b[l].astype(_BF16),
                     w_branch_c[l].astype(_BF16), w_out[l].astype(_BF16),
                     final_g if final else norm_g[l + 1], final=final, seq=seq)
        x2d, h = (res, None) if final else res
    return x2d.reshape(bsz, seq, D_MODEL)
```

```python
import functools
import math

import jax
import jax.numpy as jnp
import numpy as np
from jax import lax
from jax.experimental import pallas as pl
from jax.experimental.pallas import tpu as pltpu

D_MODEL = 2048
SEQ = 2048
DEPTH = 2
A_HEADS = 8
A_HEAD_DIM = 64
B_HEADS = 8
B_HEAD_DIM = 128
B_GROUPS = ((128, 1), (512, 4), (2048, 16))
WIDTH = 1024
C_CONV = 3
N_BRANCH = 3
Q_BLOCK = 128
N_ALIBI = A_HEADS + len(B_GROUPS) * B_HEADS
RMS_EPS = 1e-6
N_IN = 4 * WIDTH + 9 * WIDTH + WIDTH + 4 * WIDTH + N_BRANCH * D_MODEL

LANE = 128
N_SLABS = N_IN // LANE
SLABS = WIDTH // LANE
SLAB_AQ, SLAB_AK, SLAB_AV, SLAB_AZ = 0, SLABS, 2 * SLABS, 3 * SLABS
SLAB_B = 4 * SLABS
SLAB_BZ = SLAB_B + 9 * SLABS
SLAB_CB = SLAB_BZ + SLABS
SLAB_CC, SLAB_CX, SLAB_CZ = SLAB_CB + SLABS, SLAB_CB + 2 * SLABS, SLAB_CB + 3 * SLABS
SLAB_GATE = SLAB_CB + 4 * SLABS
GATE_SLABS = N_BRANCH * D_MODEL // LANE

NEG_BIG = -1e30
LOG2E = 1.4426950408889634
VMEM_LIMIT = 56 * 1024 * 1024

_BF16 = jnp.bfloat16
_F32 = jnp.float32


def _silu(z):
    return z * jax.nn.sigmoid(z)


def _rmsnorm_kernel(x_ref, g_ref, o_ref):
    x = x_ref[...]
    ms = jnp.mean(x * x, axis=-1, keepdims=True)
    o_ref[...] = (x * lax.rsqrt(ms + RMS_EPS) * g_ref[...]).astype(o_ref.dtype)


def _rmsnorm(x2d, g, *, tm=1024):
    tokens = x2d.shape[0]
    return pl.pallas_call(
        _rmsnorm_kernel,
        grid=(tokens // tm,),
        in_specs=[pl.BlockSpec((tm, D_MODEL), lambda i: (i, 0)),
                  pl.BlockSpec((1, D_MODEL), lambda i: (0, 0))],
        out_specs=pl.BlockSpec((tm, D_MODEL), lambda i: (i, 0)),
        out_shape=jax.ShapeDtypeStruct((tokens, D_MODEL), _BF16),
        compiler_params=pltpu.CompilerParams(
            dimension_semantics=("arbitrary",), vmem_limit_bytes=VMEM_LIMIT),
        name="rmsnorm",
    )(x2d, g.reshape(1, D_MODEL))


def _inproj_kernel(h_ref, w_ref, o_ref, wb_ref):
    @pl.when(pl.program_id(1) == 0)
    def _():
        wb_ref[...] = w_ref[...].astype(_BF16)

    acc = jnp.dot(h_ref[...], wb_ref[...], preferred_element_type=_F32)
    for c in range(o_ref.shape[0]):
        o_ref[c] = acc[:, c * LANE:(c + 1) * LANE].astype(o_ref.dtype)


def _inproj(h, w_in, layer, *, tm=2048, tn=1024):
    tokens = h.shape[0]
    return pl.pallas_call(
        _inproj_kernel,
        grid=(N_IN // tn, tokens // tm),
        in_specs=[
            pl.BlockSpec((tm, D_MODEL), lambda j, i: (i, 0)),
            pl.BlockSpec((None, D_MODEL, tn), lambda j, i: (layer, 0, j)),
        ],
        out_specs=pl.BlockSpec((tn // LANE, tm, LANE), lambda j, i: (j, i, 0)),
        out_shape=jax.ShapeDtypeStruct((N_SLABS, tokens, LANE), _BF16),
        scratch_shapes=[pltpu.VMEM((D_MODEL, tn), _BF16)],
        compiler_params=pltpu.CompilerParams(
            dimension_semantics=("arbitrary", "arbitrary"), vmem_limit_bytes=VMEM_LIMIT),
        name="inproj",
    )(h, w_in)


ALIBI_PIECES = 3


def _attn_a_kernel(lamv_ref, q_ref, k_ref, v_ref, z_ref, bias_ref, sg_ref, o_ref, kk_ref, va_ref,
                   *, lam_init, tq, heads):
    seq = q_ref.shape[1]
    tk = tq
    lane_s = lax.broadcasted_iota(jnp.int32, (seq, LANE), 1)
    ones_col = jnp.where(lane_s == 0, 1.0, 0.0).astype(_BF16)
    for g in range(heads):
        kk_ref[g, :, :LANE] = k_ref[g]
        kk_ref[g, :, LANE:] = bias_ref[g]
        va_ref[g, :, :LANE] = v_ref[g]
        va_ref[g, :, LANE:] = ones_col

    lane = lax.broadcasted_iota(jnp.int32, (tq, LANE), 1)
    lane2 = lax.broadcasted_iota(jnp.int32, (2 * tq, LANE), 1)
    q_ones = jnp.where(lane2 < ALIBI_PIECES, 1.0, 0.0).astype(_BF16)
    row = lax.broadcasted_iota(jnp.int32, (2 * tq, tk), 0)
    causal = lax.broadcasted_iota(jnp.int32, (2 * tq, tk), 1) <= jnp.where(row >= tq, row - tq, row)
    lv = lamv_ref[...]
    lam = (jnp.exp(jnp.sum(lv[0:1] * lv[1:2], axis=-1, keepdims=True))
           - jnp.exp(jnp.sum(lv[2:3] * lv[3:4], axis=-1, keepdims=True)) + lam_init)

    def stacked_q(g, i):
        qf = q_ref[g, pl.ds(i * tq, tq), :].astype(_F32) * (A_HEAD_DIM ** -0.5)
        qq = jnp.concatenate([jnp.where(lane < A_HEAD_DIM, qf, 0.0),
                              jnp.where(lane >= A_HEAD_DIM, qf, 0.0)], axis=0).astype(_BF16)
        return jnp.concatenate([qq, q_ones], axis=1)

    def scores(qqs, i, j):
        out = []
        for g in range(heads):
            s = lax.dot_general(qqs[g], kk_ref[g, pl.ds(j * tk, tk), :], (((1,), (1,)), ((), ())),
                                preferred_element_type=_F32)
            out.append(jnp.where(causal, s, NEG_BIG) if j == i else s)
        return out

    def finish(i, carries):
        for g in range(heads):
            acc = carries[g][1]
            o = acc[:, :LANE] / acc[:, LANE:LANE + 1]
            out = o[:tq] - lam * o[tq:]
            ms = jnp.mean(out * out, axis=-1, keepdims=True)
            oa = out * lax.rsqrt(ms + RMS_EPS) * sg_ref[...] * (1.0 - lam_init)
            rows = pl.ds(i * tq, tq)
            o_ref[g, rows, :] = (oa * _silu(z_ref[g, rows, :].astype(_F32))).astype(o_ref.dtype)

    steps = [(i, j) for i in range(seq // tq) for j in range(i + 1)]
    init = (jnp.full((2 * tq, 1), NEG_BIG, _F32), jnp.zeros((2 * tq, 2 * LANE), _F32))
    qqs = [stacked_q(g, 0) for g in range(heads)]
    s_next = scores(qqs, 0, 0)
    carries = [init] * heads
    for t, (i, j) in enumerate(steps):
        s_cur = s_next
        if t + 1 < len(steps):
            ni, nj = steps[t + 1]
            if nj == 0:
                qqs = [stacked_q(g, ni) for g in range(heads)]
            s_next = scores(qqs, ni, nj)
        probs, m_news = [], []
        for g in range(heads):
            m_new = jnp.maximum(carries[g][0], jnp.max(s_cur[g], axis=-1, keepdims=True))
            probs.append(jnp.exp((s_cur[g] - m_new).astype(_BF16)))
            m_news.append(m_new)
        new = []
        for g in range(heads):
            m, acc = carries[g]
            pv = jnp.dot(probs[g], va_ref[g, pl.ds(j * tk, tk), :], preferred_element_type=_F32)
            new.append((m_news[g], jnp.exp(m - m_news[g]) * acc + pv))
        carries = new
        if j == i:
            finish(i, carries)
            carries = [init] * heads


def _alibi_table(slopes, seq):
    rest = slopes[:A_HEADS, None] * jnp.arange(seq, dtype=_F32)[None, :]
    pieces = []
    for _ in range(ALIBI_PIECES):
        bits = lax.bitcast_convert_type(rest, jnp.uint32) & jnp.uint32(0xFFFF0000)
        part = lax.bitcast_convert_type(bits, _F32)
        pieces.append(part.astype(_BF16))
        rest = rest - part
    tab = jnp.stack(pieces, axis=-1)
    return jnp.pad(tab, ((0, 0), (0, 0), (0, LANE - ALIBI_PIECES)))


def _attn_a(proj4, slopes, lamv, subln_g, *, lam_init, tq=512, heads=2):
    bsz, seq = proj4.shape[1], proj4.shape[2]

    def slabs(first):
        return pl.BlockSpec((heads, None, seq, LANE), lambda b, h: (first // heads + h, b, 0, 0))

    return pl.pallas_call(
        functools.partial(_attn_a_kernel, lam_init=lam_init, tq=tq, heads=heads),
        grid=(bsz, A_HEADS // heads),
        in_specs=[
            pl.BlockSpec((4, A_HEAD_DIM), lambda b, h: (0, 0)),
            slabs(SLAB_AQ), slabs(SLAB_AK), slabs(SLAB_AV), slabs(SLAB_AZ),
            pl.BlockSpec((heads, seq, LANE), lambda b, h: (h, 0, 0)),
            pl.BlockSpec((1, LANE), lambda b, h: (0, 0)),
        ],
        out_specs=slabs(0),
        out_shape=jax.ShapeDtypeStruct((SLABS, bsz, seq, LANE), _BF16),
        scratch_shapes=[pltpu.VMEM((heads, seq, 2 * LANE), _BF16)] * 2,
        compiler_params=pltpu.CompilerParams(
            dimension_semantics=("arbitrary", "arbitrary"), vmem_limit_bytes=VMEM_LIMIT),
        name="attn_a",
    )(lamv, proj4, proj4, proj4, proj4, _alibi_table(slopes, seq), subln_g.reshape(1, LANE))


def _band_bias(slope, dil):
    qi = lax.broadcasted_iota(jnp.int32, (Q_BLOCK, 2 * Q_BLOCK), 0)
    ki = lax.broadcasted_iota(jnp.int32, (Q_BLOCK, 2 * Q_BLOCK), 1)
    off = Q_BLOCK + qi - ki
    valid = (off >= 0) & (off <= Q_BLOCK)
    return jnp.where(valid, -(slope * dil) * off.astype(_F32), NEG_BIG)


BAND_LOOKAHEAD = 2


def _attn_b_kernel(slopes_ref, q1_ref, k1_ref, v1_ref, q2_ref, k2_ref, v2_ref,
                   q3_ref, k3_ref, v3_ref, z_ref, o_ref, f_ref, d_ref, og_ref, lg_ref):
    seq = o_ref.shape[0]
    h = pl.program_id(1)
    group_refs = ((q1_ref, k1_ref, v1_ref), (q2_ref, k2_ref, v2_ref), (q3_ref, k3_ref, v3_ref))
    ones_blk = jnp.ones((2 * Q_BLOCK, LANE), _BF16)

    biases, blocks = [], []
    for g, (window, dil) in enumerate(B_GROUPS):
        assert window // dil == Q_BLOCK
        biases.append(_band_bias(slopes_ref[A_HEADS + g * B_HEADS + h], dil))
        sub = seq // dil
        if dil > 1:
            for part in range(3):
                f_ref[g - 1, part] = group_refs[g][part][...].astype(_F32)
                for r in range(dil):
                    d_ref[g - 1, part, pl.ds(r * sub, sub), :] = (
                        f_ref[g - 1, part, pl.ds(r, sub, stride=dil), :].astype(_BF16))
        for r in range(dil):
            for n in range(sub // Q_BLOCK):
                blocks.append((g, dil, r, n))

    def rows_of(dil, r, n):
        first = n == 0
        base = r * (seq // dil)
        k0 = n * Q_BLOCK if first else (n - 1) * Q_BLOCK
        nk = Q_BLOCK if first else 2 * Q_BLOCK
        out_rows = (pl.ds(n * Q_BLOCK, Q_BLOCK) if dil == 1
                    else pl.ds(n * Q_BLOCK * dil + r, Q_BLOCK, stride=dil))
        return pl.ds(base + n * Q_BLOCK, Q_BLOCK), pl.ds(base + k0, nk), nk, out_rows

    def load(g, part, rows):
        if g == 0:
            return group_refs[0][part][rows, :]
        return d_ref[g - 1, part, rows, :]

    def block_scores(g, dil, r, n):
        q_rows, k_rows, nk, _ = rows_of(dil, r, n)
        s = lax.dot_general(load(g, 0, q_rows), load(g, 1, k_rows), (((1,), (1,)), ((), ())),
                            preferred_element_type=_F32)
        return s * (B_HEAD_DIM ** -0.5) + biases[g][:, 2 * Q_BLOCK - nk:]

    def block_finish(s, g, dil, r, n):
        _, k_rows, nk, out_rows = rows_of(dil, r, n)
        m = jnp.max(s, axis=-1, keepdims=True)
        p = jnp.exp((s - m).astype(_BF16))
        v_ones = jnp.concatenate([load(g, 2, k_rows), ones_blk[:nk]], axis=1)
        pv = jnp.dot(p, v_ones, preferred_element_type=_F32)
        l = pv[:, LANE:]
        og_ref[g, out_rows, :] = pv[:, :LANE] / l
        lg_ref[g, out_rows, :] = m + jnp.log(l)

    pending = []
    for t in range(len(blocks) + BAND_LOOKAHEAD):
        if t < len(blocks):
            pending.append(block_scores(*blocks[t]))
        if t >= BAND_LOOKAHEAD:
            block_finish(pending.pop(0), *blocks[t - BAND_LOOKAHEAD])

    rows = 256
    for c in range(seq // rows):
        sl = pl.ds(c * rows, rows)
        l0, l1, l2 = lg_ref[0, sl, :], lg_ref[1, sl, :], lg_ref[2, sl, :]
        mx = jnp.maximum(jnp.maximum(l0, l1), l2)
        w0, w1, w2 = jnp.exp(l0 - mx), jnp.exp(l1 - mx), jnp.exp(l2 - mx)
        ob = (w0 * og_ref[0, sl, :] + w1 * og_ref[1, sl, :] + w2 * og_ref[2, sl, :]) / (w0 + w1 + w2)
        o_ref[sl, :] = (ob * _silu(z_ref[sl, :].astype(_F32))).astype(o_ref.dtype)


def _attn_b(proj4, slopes):
    bsz, seq = proj4.shape[1], proj4.shape[2]

    def slab(first):
        return pl.BlockSpec((None, None, seq, LANE), lambda b, h: (first + h, b, 0, 0))

    in_specs = [pl.BlockSpec(memory_space=pltpu.SMEM)]
    for g in range(len(B_GROUPS)):
        for part in range(3):
            in_specs.append(slab(SLAB_B + (3 * g + part) * SLABS))
    in_specs.append(slab(SLAB_BZ))
    return pl.pallas_call(
        _attn_b_kernel,
        grid=(bsz, B_HEADS),
        in_specs=in_specs,
        out_specs=pl.BlockSpec((None, None, seq, LANE), lambda b, h: (h, b, 0, 0)),
        out_shape=jax.ShapeDtypeStruct((SLABS, bsz, seq, LANE), _BF16),
        scratch_shapes=[pltpu.VMEM((2, 3, seq, LANE), _F32), pltpu.VMEM((2, 3, seq, LANE), _BF16),
                        pltpu.VMEM((3, seq, LANE), _F32), pltpu.VMEM((3, seq, LANE), _F32)],
        compiler_params=pltpu.CompilerParams(
            dimension_semantics=("arbitrary", "arbitrary"), vmem_limit_bytes=VMEM_LIMIT),
        name="attn_b",
    )(slopes, *([proj4] * 10))


CONV_HALO = 8

def _merge_kernel(ya_ref, yb_ref, cbc_ref, cxz_ref, cw_ref, gp_ref, gb_ref, x_ref, wa_ref, wb_ref,
                  wc_ref, wo_ref, ng_ref, *rest, final, tiles_per_seq):
    if final:
        o_ref, cat_ref, mrg_ref, halo_ref = rest
    else:
        o_ref, h_ref, cat_ref, mrg_ref, halo_ref = rest
    tm = x_ref.shape[0]
    branch = []
    for y_ref, w_ref in ((ya_ref, wa_ref), (yb_ref, wb_ref)):
        for c in range(SLABS):
            cat_ref[:, c * LANE:(c + 1) * LANE] = y_ref[c]
        branch.append(jnp.dot(cat_ref[...], w_ref[...], preferred_element_type=_F32))

    seq_start = pl.program_id(0) % tiles_per_seq == 0
    for c in range(SLABS):
        cols = slice(c * LANE, (c + 1) * LANE)
        u = cbc_ref[SLABS + c].astype(_F32) * cxz_ref[c].astype(_F32)
        ext = jnp.concatenate([jnp.where(seq_start, 0.0, halo_ref[c]), u], axis=0)
        halo_ref[c] = u[tm - CONV_HALO:]
        u1 = pltpu.roll(ext, 1, 0)[CONV_HALO:]
        u2 = pltpu.roll(ext, 2, 0)[CONV_HALO:]
        conv = cw_ref[0:1, cols] * u2 + cw_ref[1:2, cols] * u1 + cw_ref[2:3, cols] * u
        yc = cbc_ref[c].astype(_F32) * conv * _silu(cxz_ref[SLABS + c].astype(_F32))
        cat_ref[:, cols] = yc.astype(_BF16)
    branch.append(jnp.dot(cat_ref[...], wc_ref[...], preferred_element_type=_F32))
    per_branch = D_MODEL // LANE
    for c in range(per_branch):
        cols = slice(c * LANE, (c + 1) * LANE)
        tot = None
        for n in range(N_BRANCH):
            gcols = slice(n * D_MODEL + c * LANE, n * D_MODEL + (c + 1) * LANE)
            gate = jax.nn.sigmoid(gp_ref[n * per_branch + c].astype(_F32) + gb_ref[:, gcols])
            term = gate * branch[n][:, cols]
            tot = term if tot is None else tot + term
        mrg_ref[:, cols] = tot.astype(_BF16)
    out = x_ref[...] + jnp.dot(mrg_ref[...], wo_ref[...], preferred_element_type=_F32)
    ms = jnp.mean(out * out, axis=-1, keepdims=True)
    normed = out * lax.rsqrt(ms + RMS_EPS) * ng_ref[...]
    if final:
        o_ref[...] = normed
    else:
        o_ref[...] = out
        h_ref[...] = normed.astype(h_ref.dtype)


def _merge(ya, yb, proj, conv_w, gate_b, x2d, wa, wb, wc, wo, next_g, *, final, seq, tm=256):
    tokens = x2d.shape[0]
    assert seq % tm == 0 and SLAB_CB % (2 * SLABS) == 0 and SLAB_CX % (2 * SLABS) == 0
    row_spec = pl.BlockSpec((tm, D_MODEL), lambda i: (i, 0))
    if final:
        out_specs, out_shape = row_spec, jax.ShapeDtypeStruct((tokens, D_MODEL), _F32)
    else:
        out_specs = (row_spec, row_spec)
        out_shape = (jax.ShapeDtypeStruct((tokens, D_MODEL), _F32),
                     jax.ShapeDtypeStruct((tokens, D_MODEL), _BF16))
    const = lambda shape: pl.BlockSpec(shape, lambda i: (0,) * len(shape),
                                       pipeline_mode=pl.Buffered(1))
    yspec = pl.BlockSpec((SLABS, tm, LANE), lambda i: (0, i, 0))
    return pl.pallas_call(
        functools.partial(_merge_kernel, final=final, tiles_per_seq=seq // tm),
        grid=(tokens // tm,),
        in_specs=[
            yspec, yspec,
            pl.BlockSpec((2 * SLABS, tm, LANE), lambda i: (SLAB_CB // (2 * SLABS), i, 0)),
            pl.BlockSpec((2 * SLABS, tm, LANE), lambda i: (SLAB_CX // (2 * SLABS), i, 0)),
            const((C_CONV, WIDTH)),
            pl.BlockSpec((GATE_SLABS, tm, LANE), lambda i: (SLAB_GATE // GATE_SLABS, i, 0)),
            const((1, N_BRANCH * D_MODEL)),
            pl.BlockSpec((tm, D_MODEL), lambda i: (i, 0)),
            const((WIDTH, D_MODEL)), const((WIDTH, D_MODEL)), const((WIDTH, D_MODEL)),
            const((D_MODEL, D_MODEL)),
            const((1, D_MODEL)),
        ],
        out_specs=out_specs,
        out_shape=out_shape,
        scratch_shapes=[pltpu.VMEM((tm, WIDTH), _BF16), pltpu.VMEM((tm, D_MODEL), _BF16),
                        pltpu.VMEM((SLABS, CONV_HALO, LANE), _F32)],
        compiler_params=pltpu.CompilerParams(
            dimension_semantics=("arbitrary",), vmem_limit_bytes=VMEM_LIMIT),
        name="merge",
    )(ya, yb, proj, proj, conv_w, proj, gate_b.reshape(1, N_BRANCH * D_MODEL), x2d, wa, wb, wc, wo,
      next_g.reshape(1, D_MODEL))


def kernel(x, norm_g, w_in, gate_b, lambda_q1, lambda_k1, lambda_q2, lambda_k2, subln_g, conv_w,
           w_branch_a, w_branch_b, w_branch_c, w_out, final_g):
    bsz, seq, _ = x.shape
    assert seq == SEQ and SLAB_GATE % GATE_SLABS == 0
    tokens = bsz * seq
    slopes = jnp.asarray(2.0 ** (-8.0 * np.arange(1, N_ALIBI + 1) / N_ALIBI), dtype=_F32)
    x2d = x.reshape(tokens, D_MODEL)
    h = _rmsnorm(x2d, norm_g[0])
    for l in range(DEPTH):
        final = l == DEPTH - 1
        lam_init = 0.8 - 0.6 * math.exp(-0.3 * l)
        lamv = jnp.stack([lambda_q1[l], lambda_k1[l], lambda_q2[l], lambda_k2[l]])
        proj = _inproj(h, w_in, l)
        proj4 = proj.reshape(N_SLABS, bsz, seq, LANE)
        ya = _attn_a(proj4, slopes, lamv, subln_g[l], lam_init=lam_init)
        yb = _attn_b(proj4, slopes)
        to3 = lambda y: y.reshape(SLABS, tokens, LANE)
        res = _merge(to3(ya), to3(yb), proj, conv_w[l], gate_b[l], x2d,
                     w_branch_a[l].astype(_BF16), w_branch_b[l].astype(_BF16),
                     w_branch_c[l].astype(_BF16), w_out[l].astype(_BF16),
                     final_g if final else norm_g[l + 1], final=final, seq=seq)
        x2d, h = (res, None) if final else res
    return x2d.reshape(bsz, seq, D_MODEL)
```

```python
import functools
import math

import jax
import jax.numpy as jnp
import numpy as np
from jax import lax
from jax.experimental import pallas as pl
from jax.experimental.pallas import tpu as pltpu

D_MODEL = 2048
SEQ = 2048
DEPTH = 2
A_HEADS = 8
A_HEAD_DIM = 64
B_HEADS = 8
B_HEAD_DIM = 128
B_GROUPS = ((128, 1), (512, 4), (2048, 16))
WIDTH = 1024
C_CONV = 3
N_BRANCH = 3
Q_BLOCK = 128
N_ALIBI = A_HEADS + len(B_GROUPS) * B_HEADS
RMS_EPS = 1e-6
N_IN = 4 * WIDTH + 9 * WIDTH + WIDTH + 4 * WIDTH + N_BRANCH * D_MODEL

LANE = 128
N_SLABS = N_IN // LANE
SLABS = WIDTH // LANE
SLAB_AQ, SLAB_AK, SLAB_AV, SLAB_AZ = 0, SLABS, 2 * SLABS, 3 * SLABS
SLAB_B = 4 * SLABS
SLAB_BZ = SLAB_B + 9 * SLABS
SLAB_CB = SLAB_BZ + SLABS
SLAB_CC, SLAB_CX, SLAB_CZ = SLAB_CB + SLABS, SLAB_CB + 2 * SLABS, SLAB_CB + 3 * SLABS
SLAB_GATE = SLAB_CB + 4 * SLABS
GATE_SLABS = N_BRANCH * D_MODEL // LANE

NEG_BIG = -1e30
LOG2E = 1.4426950408889634
VMEM_LIMIT = 56 * 1024 * 1024

_BF16 = jnp.bfloat16
_F32 = jnp.float32


def _silu(z):
    return z * jax.nn.sigmoid(z)


def _rmsnorm_kernel(x_ref, g_ref, o_ref):
    x = x_ref[...]
    ms = jnp.mean(x * x, axis=-1, keepdims=True)
    o_ref[...] = (x * lax.rsqrt(ms + RMS_EPS) * g_ref[...]).astype(o_ref.dtype)


def _rmsnorm(x2d, g, *, tm=1024):
    tokens = x2d.shape[0]
    return pl.pallas_call(
        _rmsnorm_kernel,
        grid=(tokens // tm,),
        in_specs=[pl.BlockSpec((tm, D_MODEL), lambda i: (i, 0)),
                  pl.BlockSpec((1, D_MODEL), lambda i: (0, 0))],
        out_specs=pl.BlockSpec((tm, D_MODEL), lambda i: (i, 0)),
        out_shape=jax.ShapeDtypeStruct((tokens, D_MODEL), _BF16),
        compiler_params=pltpu.CompilerParams(
            dimension_semantics=("arbitrary",), vmem_limit_bytes=VMEM_LIMIT),
        name="rmsnorm",
    )(x2d, g.reshape(1, D_MODEL))


def _inproj_kernel(h_ref, w_ref, o_ref, wb_ref):
    @pl.when(pl.program_id(1) == 0)
    def _():
        wb_ref[...] = w_ref[...].astype(_BF16)

    acc = jnp.dot(h_ref[...], wb_ref[...], preferred_element_type=_F32)
    for c in range(o_ref.shape[0]):
        o_ref[c] = acc[:, c * LANE:(c + 1) * LANE].astype(o_ref.dtype)


def _inproj(h, w_in, layer, *, tm=2048, tn=1024):
    tokens = h.shape[0]
    return pl.pallas_call(
        _inproj_kernel,
        grid=(N_IN // tn, tokens // tm),
        in_specs=[
            pl.BlockSpec((tm, D_MODEL), lambda j, i: (i, 0)),
            pl.BlockSpec((None, D_MODEL, tn), lambda j, i: (layer, 0, j)),
        ],
        out_specs=pl.BlockSpec((tn // LANE, tm, LANE), lambda j, i: (j, i, 0)),
        out_shape=jax.ShapeDtypeStruct((N_SLABS, tokens, LANE), _BF16),
        scratch_shapes=[pltpu.VMEM((D_MODEL, tn), _BF16)],
        compiler_params=pltpu.CompilerParams(
            dimension_semantics=("arbitrary", "arbitrary"), vmem_limit_bytes=VMEM_LIMIT),
        name="inproj",
    )(h, w_in)


ALIBI_PIECES = 3


def _attn_a_kernel(lamv_ref, q_ref, k_ref, v_ref, z_ref, bias_ref, sg_ref, o_ref, kk_ref, va_ref,
                   *, lam_init, tq, heads):
    seq = q_ref.shape[1]
    lane_s = lax.broadcasted_iota(jnp.int32, (seq, LANE), 1)
    ones_col = jnp.where(lane_s == 0, 1.0, 0.0).astype(_BF16)
    for g in range(heads):
        kk_ref[g, :, :LANE] = k_ref[g]
        kk_ref[g, :, LANE:] = bias_ref[g]
        va_ref[g, :, :LANE] = v_ref[g]
        va_ref[g, :, LANE:] = ones_col

    lane = lax.broadcasted_iota(jnp.int32, (tq, LANE), 1)
    lane2 = lax.broadcasted_iota(jnp.int32, (2 * tq, LANE), 1)
    q_ones = jnp.where(lane2 < ALIBI_PIECES, 1.0, 0.0).astype(_BF16)
    row = lax.broadcasted_iota(jnp.int32, (2 * tq, tq), 0)
    causal = lax.broadcasted_iota(jnp.int32, (2 * tq, tq), 1) <= jnp.where(row >= tq, row - tq, row)
    lv = lamv_ref[...]
    lam = (jnp.exp(jnp.sum(lv[0:1] * lv[1:2], axis=-1, keepdims=True))
           - jnp.exp(jnp.sum(lv[2:3] * lv[3:4], axis=-1, keepdims=True)) + lam_init)

    def stacked_q(g, i):
        qf = q_ref[g, pl.ds(i * tq, tq), :].astype(_F32) * (A_HEAD_DIM ** -0.5)
        qq = jnp.concatenate([jnp.where(lane < A_HEAD_DIM, qf, 0.0),
                              jnp.where(lane >= A_HEAD_DIM, qf, 0.0)], axis=0).astype(_BF16)
        return jnp.concatenate([qq, q_ones], axis=1)

    def scores(i):
        out = []
        for g in range(heads):
            s = lax.dot_general(stacked_q(g, i), kk_ref[g, pl.ds(0, (i + 1) * tq), :],
                                (((1,), (1,)), ((), ())), preferred_element_type=_F32)
            own = jnp.where(causal, s[:, i * tq:], NEG_BIG)
            out.append(own if i == 0 else jnp.concatenate([s[:, :i * tq], own], axis=1))
        return out

    def finish(i, s_blk):
        probs = []
        for g in range(heads):
            m = jnp.max(s_blk[g], axis=-1, keepdims=True)
            probs.append(jnp.exp((s_blk[g] - m).astype(_BF16)))
        for g in range(heads):
            pv = jnp.dot(probs[g], va_ref[g, pl.ds(0, (i + 1) * tq), :],
                         preferred_element_type=_F32)
            o = pv[:, :LANE] / pv[:, LANE:LANE + 1]
            out = o[:tq] - lam * o[tq:]
            ms = jnp.mean(out * out, axis=-1, keepdims=True)
            oa = out * lax.rsqrt(ms + RMS_EPS) * sg_ref[...] * (1.0 - lam_init)
            rows = pl.ds(i * tq, tq)
            o_ref[g, rows, :] = (oa * _silu(z_ref[g, rows, :].astype(_F32))).astype(o_ref.dtype)

    n_blocks = seq // tq
    s_next = scores(0)
    for i in range(n_blocks):
        s_cur = s_next
        if i + 1 < n_blocks:
            s_next = scores(i + 1)
        finish(i, s_cur)


def _alibi_table(slopes, seq):
    rest = slopes[:A_HEADS, None] * jnp.arange(seq, dtype=_F32)[None, :]
    pieces = []
    for _ in range(ALIBI_PIECES):
        bits = lax.bitcast_convert_type(rest, jnp.uint32) & jnp.uint32(0xFFFF0000)
        part = lax.bitcast_convert_type(bits, _F32)
        pieces.append(part.astype(_BF16))
        rest = rest - part
    tab = jnp.stack(pieces, axis=-1)
    return jnp.pad(tab, ((0, 0), (0, 0), (0, LANE - ALIBI_PIECES)))


def _attn_a(proj4, slopes, lamv, subln_g, *, lam_init, tq=256, heads=2):
    bsz, seq = proj4.shape[1], proj4.shape[2]

    def slabs(first):
        return pl.BlockSpec((heads, None, seq, LANE), lambda b, h: (first // heads + h, b, 0, 0))

    return pl.pallas_call(
        functools.partial(_attn_a_kernel, lam_init=lam_init, tq=tq, heads=heads),
        grid=(bsz, A_HEADS // heads),
        in_specs=[
            pl.BlockSpec((4, A_HEAD_DIM), lambda b, h: (0, 0)),
            slabs(SLAB_AQ), slabs(SLAB_AK), slabs(SLAB_AV), slabs(SLAB_AZ),
            pl.BlockSpec((heads, seq, LANE), lambda b, h: (h, 0, 0)),
            pl.BlockSpec((1, LANE), lambda b, h: (0, 0)),
        ],
        out_specs=slabs(0),
        out_shape=jax.ShapeDtypeStruct((SLABS, bsz, seq, LANE), _BF16),
        scratch_shapes=[pltpu.VMEM((heads, seq, 2 * LANE), _BF16)] * 2,
        compiler_params=pltpu.CompilerParams(
            dimension_semantics=("arbitrary", "arbitrary"), vmem_limit_bytes=VMEM_LIMIT),
        name="attn_a",
    )(lamv, proj4, proj4, proj4, proj4, _alibi_table(slopes, seq), subln_g.reshape(1, LANE))


def _band_bias(slope, dil):
    qi = lax.broadcasted_iota(jnp.int32, (Q_BLOCK, 2 * Q_BLOCK), 0)
    ki = lax.broadcasted_iota(jnp.int32, (Q_BLOCK, 2 * Q_BLOCK), 1)
    off = Q_BLOCK + qi - ki
    valid = (off >= 0) & (off <= Q_BLOCK)
    return jnp.where(valid, -(slope * dil) * off.astype(_F32), NEG_BIG)


BAND_LOOKAHEAD = 2


def _attn_b_kernel(slopes_ref, q1_ref, k1_ref, v1_ref, q2_ref, k2_ref, v2_ref,
                   q3_ref, k3_ref, v3_ref, z_ref, o_ref, f_ref, og_ref, lg_ref):
    seq = o_ref.shape[0]
    h = pl.program_id(1)
    group_refs = ((q1_ref, k1_ref, v1_ref), (q2_ref, k2_ref, v2_ref), (q3_ref, k3_ref, v3_ref))
    ones_blk = jnp.ones((2 * Q_BLOCK, LANE), _BF16)

    biases, blocks = [], []
    for g, (window, dil) in enumerate(B_GROUPS):
        assert window // dil == Q_BLOCK
        biases.append(_band_bias(slopes_ref[A_HEADS + g * B_HEADS + h], dil))
        qr, kr, vr = group_refs[g]
        if dil > 1:
            f_ref[g - 1, 0] = qr[...].astype(_F32)
            f_ref[g - 1, 1] = kr[...].astype(_F32)
            f_ref[g - 1, 2] = vr[...].astype(_F32)
        for r in range(dil):
            for n in range(seq // dil // Q_BLOCK):
                blocks.append((g, dil, r, n))

    def rows_of(dil, r, n):
        first = n == 0
        k0 = n * Q_BLOCK if first else (n - 1) * Q_BLOCK
        nk = Q_BLOCK if first else 2 * Q_BLOCK
        if dil == 1:
            return pl.ds(n * Q_BLOCK, Q_BLOCK), pl.ds(k0, nk), nk
        return (pl.ds(n * Q_BLOCK * dil + r, Q_BLOCK, stride=dil),
                pl.ds(k0 * dil + r, nk, stride=dil), nk)

    def load(g, part, rows):
        if g == 0:
            return group_refs[0][part][rows, :]
        return f_ref[g - 1, part, rows, :].astype(_BF16)

    def block_scores(g, dil, r, n):
        q_rows, k_rows, nk = rows_of(dil, r, n)
        s = lax.dot_general(load(g, 0, q_rows), load(g, 1, k_rows), (((1,), (1,)), ((), ())),
                            preferred_element_type=_F32)
        return s * (B_HEAD_DIM ** -0.5) + biases[g][:, 2 * Q_BLOCK - nk:]

    def block_finish(s, g, dil, r, n):
        q_rows, k_rows, nk = rows_of(dil, r, n)
        m = jnp.max(s, axis=-1, keepdims=True)
        p = jnp.exp((s - m).astype(_BF16))
        v_ones = jnp.concatenate([load(g, 2, k_rows), ones_blk[:nk]], axis=1)
        pv = jnp.dot(p, v_ones, preferred_element_type=_F32)
        l = pv[:, LANE:]
        og_ref[g, q_rows, :] = pv[:, :LANE] / l
        lg_ref[g, q_rows, :] = m + jnp.log(l)

    pending = []
    for t in range(len(blocks) + BAND_LOOKAHEAD):
        if t < len(blocks):
            pending.append(block_scores(*blocks[t]))
        if t >= BAND_LOOKAHEAD:
            block_finish(pending.pop(0), *blocks[t - BAND_LOOKAHEAD])

    rows = 256
    for c in range(seq // rows):
        sl = pl.ds(c * rows, rows)
        l0, l1, l2 = lg_ref[0, sl, :], lg_ref[1, sl, :], lg_ref[2, sl, :]
        mx = jnp.maximum(jnp.maximum(l0, l1), l2)
        w0, w1, w2 = jnp.exp(l0 - mx), jnp.exp(l1 - mx), jnp.exp(l2 - mx)
        ob = (w0 * og_ref[0, sl, :] + w1 * og_ref[1, sl, :] + w2 * og_ref[2, sl, :]) / (w0 + w1 + w2)
        o_ref[sl, :] = (ob * _silu(z_ref[sl, :].astype(_F32))).astype(o_ref.dtype)


def _attn_b(proj4, slopes):
    bsz, seq = proj4.shape[1], proj4.shape[2]

    def slab(first):
        return pl.BlockSpec((None, None, seq, LANE), lambda b, h: (first + h, b, 0, 0))

    in_specs = [pl.BlockSpec(memory_space=pltpu.SMEM)]
    for g in range(len(B_GROUPS)):
        for part in range(3):
            in_specs.append(slab(SLAB_B + (3 * g + part) * SLABS))
    in_specs.append(slab(SLAB_BZ))
    return pl.pallas_call(
        _attn_b_kernel,
        grid=(bsz, B_HEADS),
        in_specs=in_specs,
        out_specs=pl.BlockSpec((None, None, seq, LANE), lambda b, h: (h, b, 0, 0)),
        out_shape=jax.ShapeDtypeStruct((SLABS, bsz, seq, LANE), _BF16),
        scratch_shapes=[pltpu.VMEM((2, 3, seq, LANE), _F32), pltpu.VMEM((3, seq, LANE), _F32),
                        pltpu.VMEM((3, seq, LANE), _F32)],
        compiler_params=pltpu.CompilerParams(
            dimension_semantics=("arbitrary", "arbitrary"), vmem_limit_bytes=VMEM_LIMIT),
        name="attn_b",
    )(slopes, *([proj4] * 10))


CONV_HALO = 8


def _merge_kernel(ya_ref, yb_ref, cbc_ref, cxz_ref, cw_ref, gp_ref, gb_ref, x_ref, wa_ref, wb_ref,
                  wc_ref, wo_ref, ng_ref, *rest, final, tiles_per_seq):
    if final:
        o_ref, cat_ref, mrg_ref, halo_ref = rest
    else:
        o_ref, h_ref, cat_ref, mrg_ref, halo_ref = rest
    tm = x_ref.shape[0]
    branch = []
    for y_ref, w_ref in ((ya_ref, wa_ref), (yb_ref, wb_ref)):
        for c in range(SLABS):
            cat_ref[:, c * LANE:(c + 1) * LANE] = y_ref[c]
        branch.append(jnp.dot(cat_ref[...], w_ref[...], preferred_element_type=_F32))

    seq_start = pl.program_id(0) % tiles_per_seq == 0
    for c in range(SLABS):
        cols = slice(c * LANE, (c + 1) * LANE)
        u = cbc_ref[SLABS + c].astype(_F32) * cxz_ref[c].astype(_F32)
        ext = jnp.concatenate([jnp.where(seq_start, 0.0, halo_ref[c]), u], axis=0)
        halo_ref[c] = u[tm - CONV_HALO:]
        u1 = pltpu.roll(ext, 1, 0)[CONV_HALO:]
        u2 = pltpu.roll(ext, 2, 0)[CONV_HALO:]
        conv = cw_ref[0:1, cols] * u2 + cw_ref[1:2, cols] * u1 + cw_ref[2:3, cols] * u
        yc = cbc_ref[c].astype(_F32) * conv * _silu(cxz_ref[SLABS + c].astype(_F32))
        cat_ref[:, cols] = yc.astype(_BF16)
    branch.append(jnp.dot(cat_ref[...], wc_ref[...], preferred_element_type=_F32))
    per_branch = D_MODEL // LANE
    for c in range(per_branch):
        cols = slice(c * LANE, (c + 1) * LANE)
        tot = None
        for n in range(N_BRANCH):
            gcols = slice(n * D_MODEL + c * LANE, n * D_MODEL + (c + 1) * LANE)
            gate = jax.nn.sigmoid(gp_ref[n * per_branch + c].astype(_F32) + gb_ref[:, gcols])
            term = gate * branch[n][:, cols]
            tot = term if tot is None else tot + term
        mrg_ref[:, cols] = tot.astype(_BF16)
    out = x_ref[...] + jnp.dot(mrg_ref[...], wo_ref[...], preferred_element_type=_F32)
    ms = jnp.mean(out * out, axis=-1, keepdims=True)
    normed = out * lax.rsqrt(ms + RMS_EPS) * ng_ref[...]
    if final:
        o_ref[...] = normed
    else:
        o_ref[...] = out
        h_ref[...] = normed.astype(h_ref.dtype)


def _merge(ya, yb, proj, conv_w, gate_b, x2d, wa, wb, wc, wo, next_g, *, final, seq, tm=256):
    tokens = x2d.shape[0]
    assert seq % tm == 0 and SLAB_CB % (2 * SLABS) == 0 and SLAB_CX % (2 * SLABS) == 0
    row_spec = pl.BlockSpec((tm, D_MODEL), lambda i: (i, 0))
    if final:
        out_specs, out_shape = row_spec, jax.ShapeDtypeStruct((tokens, D_MODEL), _F32)
    else:
        out_specs = (row_spec, row_spec)
        out_shape = (jax.ShapeDtypeStruct((tokens, D_MODEL), _F32),
                     jax.ShapeDtypeStruct((tokens, D_MODEL), _BF16))
    const = lambda shape: pl.BlockSpec(shape, lambda i: (0,) * len(shape),
                                       pipeline_mode=pl.Buffered(1))
    yspec = pl.BlockSpec((SLABS, tm, LANE), lambda i: (0, i, 0))
    return pl.pallas_call(
        functools.partial(_merge_kernel, final=final, tiles_per_seq=seq // tm),
        grid=(tokens // tm,),
        in_specs=[
            yspec, yspec,
            pl.BlockSpec((2 * SLABS, tm, LANE), lambda i: (SLAB_CB // (2 * SLABS), i, 0)),
            pl.BlockSpec((2 * SLABS, tm, LANE), lambda i: (SLAB_CX // (2 * SLABS), i, 0)),
            const((C_CONV, WIDTH)),
            pl.BlockSpec((GATE_SLABS, tm, LANE), lambda i: (SLAB_GATE // GATE_SLABS, i, 0)),
            const((1, N_BRANCH * D_MODEL)),
            pl.BlockSpec((tm, D_MODEL), lambda i: (i, 0)),
            const((WIDTH, D_MODEL)), const((WIDTH, D_MODEL)), const((WIDTH, D_MODEL)),
            const((D_MODEL, D_MODEL)),
            const((1, D_MODEL)),
        ],
        out_specs=out_specs,
        out_shape=out_shape,
        scratch_shapes=[pltpu.VMEM((tm, WIDTH), _BF16), pltpu.VMEM((tm, D_MODEL), _BF16),
                        pltpu.VMEM((SLABS, CONV_HALO, LANE), _F32)],
        compiler_params=pltpu.CompilerParams(
            dimension_semantics=("arbitrary",), vmem_limit_bytes=VMEM_LIMIT),
        name="merge",
    )(ya, yb, proj, proj, conv_w, proj, gate_b.reshape(1, N_BRANCH * D_MODEL), x2d, wa, wb, wc, wo,
      next_g.reshape(1, D_MODEL))


def kernel(x, norm_g, w_in, gate_b, lambda_q1, lambda_k1, lambda_q2, lambda_k2, subln_g, conv_w,
           w_branch_a, w_branch_b, w_branch_c, w_out, final_g):
    bsz, seq, _ = x.shape
    assert seq == SEQ and SLAB_GATE % GATE_SLABS == 0
    tokens = bsz * seq
    slopes = jnp.asarray(2.0 ** (-8.0 * np.arange(1, N_ALIBI + 1) / N_ALIBI), dtype=_F32)
    x2d = x.reshape(tokens, D_MODEL)
    h = _rmsnorm(x2d, norm_g[0])
    for l in range(DEPTH):
        final = l == DEPTH - 1
        lam_init = 0.8 - 0.6 * math.exp(-0.3 * l)
        lamv = jnp.stack([lambda_q1[l], lambda_k1[l], lambda_q2[l], lambda_k2[l]])
        proj = _inproj(h, w_in, l)
        proj4 = proj.reshape(N_SLABS, bsz, seq, LANE)
        ya = _attn_a(proj4, slopes, lamv, subln_g[l], lam_init=lam_init)
        yb = _attn_b(proj4, slopes)
        to3 = lambda y: y.reshape(SLABS, tokens, LANE)
        res = _merge(to3(ya), to3(yb), proj, conv_w[l], gate_b[l], x2d,
                     w_branch_a[l].astype(_BF16), w_branch_b[l].astype(_BF16),
                     w_branch_c[l].astype(_BF16), w_out[l].astype(_BF16),
                     final_g if final else norm_g[l + 1], final=final, seq=seq)
        x2d, h = (res, None) if final else res
    return x2d.reshape(bsz, seq, D_MODEL)
```

```python
import functools
import math

import jax
import jax.numpy as jnp
import numpy as np
from jax import lax
from jax.experimental import pallas as pl
from jax.experimental.pallas import tpu as pltpu

D_MODEL = 2048
SEQ = 2048
DEPTH = 2
A_HEADS = 8
A_HEAD_DIM = 64
B_HEADS = 8
B_HEAD_DIM = 128
B_GROUPS = ((128, 1), (512, 4), (2048, 16))
WIDTH = 1024
C_CONV = 3
N_BRANCH = 3
Q_BLOCK = 128
N_ALIBI = A_HEADS + len(B_GROUPS) * B_HEADS
RMS_EPS = 1e-6
N_IN = 4 * WIDTH + 9 * WIDTH + WIDTH + 4 * WIDTH + N_BRANCH * D_MODEL

LANE = 128
N_SLABS = N_IN // LANE
SLABS = WIDTH // LANE
SLAB_AQ, SLAB_AK, SLAB_AV, SLAB_AZ = 0, SLABS, 2 * SLABS, 3 * SLABS
SLAB_B = 4 * SLABS
SLAB_BZ = SLAB_B + 9 * SLABS
SLAB_CB = SLAB_BZ + SLABS
SLAB_CC, SLAB_CX, SLAB_CZ = SLAB_CB + SLABS, SLAB_CB + 2 * SLABS, SLAB_CB + 3 * SLABS
SLAB_GATE = SLAB_CB + 4 * SLABS
GATE_SLABS = N_BRANCH * D_MODEL // LANE

NEG_BIG = -1e30
VMEM_LIMIT = 56 * 1024 * 1024

_BF16 = jnp.bfloat16
_F32 = jnp.float32


def _silu(z):
    return z * jax.nn.sigmoid(z)


def _rmsnorm_kernel(x_ref, g_ref, o_ref):
    x = x_ref[...]
    ms = jnp.mean(x * x, axis=-1, keepdims=True)
    o_ref[...] = (x * lax.rsqrt(ms + RMS_EPS) * g_ref[...]).astype(o_ref.dtype)


def _rmsnorm(x2d, g, *, tm=1024):
    tokens = x2d.shape[0]
    return pl.pallas_call(
        _rmsnorm_kernel,
        grid=(tokens // tm,),
        in_specs=[pl.BlockSpec((tm, D_MODEL), lambda i: (i, 0)),
                  pl.BlockSpec((1, D_MODEL), lambda i: (0, 0))],
        out_specs=pl.BlockSpec((tm, D_MODEL), lambda i: (i, 0)),
        out_shape=jax.ShapeDtypeStruct((tokens, D_MODEL), _BF16),
        compiler_params=pltpu.CompilerParams(
            dimension_semantics=("arbitrary",), vmem_limit_bytes=VMEM_LIMIT),
        name="rmsnorm",
    )(x2d, g.reshape(1, D_MODEL))


def _inproj_kernel(h_ref, w_ref, o_ref, wb_ref):
    @pl.when(pl.program_id(1) == 0)
    def _():
        wb_ref[...] = w_ref[...].astype(_BF16)

    acc = jnp.dot(h_ref[...], wb_ref[...], preferred_element_type=_F32)
    for c in range(o_ref.shape[0]):
        o_ref[c] = acc[:, c * LANE:(c + 1) * LANE].astype(o_ref.dtype)


def _inproj(h, w_in, layer, *, tm=2048, tn=1024):
    tokens = h.shape[0]
    return pl.pallas_call(
        _inproj_kernel,
        grid=(N_IN // tn, tokens // tm),
        in_specs=[
            pl.BlockSpec((tm, D_MODEL), lambda j, i: (i, 0)),
            pl.BlockSpec((None, D_MODEL, tn), lambda j, i: (layer, 0, j)),
        ],
        out_specs=pl.BlockSpec((tn // LANE, tm, LANE), lambda j, i: (j, i, 0)),
        out_shape=jax.ShapeDtypeStruct((N_SLABS, tokens, LANE), _BF16),
        scratch_shapes=[pltpu.VMEM((D_MODEL, tn), _BF16)],
        compiler_params=pltpu.CompilerParams(
            dimension_semantics=("arbitrary", "arbitrary"), vmem_limit_bytes=VMEM_LIMIT),
        name="inproj",
    )(h, w_in)


ALIBI_PIECES = 3


def _attn_a_kernel(lamv_ref, q_ref, k_ref, v_ref, z_ref, bias_ref, sg_ref, o_ref, kk_ref, va_ref,
                   *, lam_init, tq, heads):
    seq = q_ref.shape[1]
    ones_col = jnp.ones((seq, LANE), _BF16)
    for g in range(heads):
        kk_ref[g, :, :LANE] = k_ref[g]
        kk_ref[g, :, LANE:] = bias_ref[g]
        va_ref[g, :, :LANE] = v_ref[g]
        va_ref[g, :, LANE:] = ones_col

    lane = lax.broadcasted_iota(jnp.int32, (tq, LANE), 1)
    lane2 = lax.broadcasted_iota(jnp.int32, (2 * tq, LANE), 1)
    q_ones = jnp.where(lane2 < ALIBI_PIECES, 1.0, 0.0).astype(_BF16)
    row = lax.broadcasted_iota(jnp.int32, (2 * tq, tq), 0)
    causal = lax.broadcasted_iota(jnp.int32, (2 * tq, tq), 1) <= jnp.where(row >= tq, row - tq, row)
    lv = lamv_ref[...]
    lam = (jnp.exp(jnp.sum(lv[0:1] * lv[1:2], axis=-1, keepdims=True))
           - jnp.exp(jnp.sum(lv[2:3] * lv[3:4], axis=-1, keepdims=True)) + lam_init)

    def stacked_q(g, i):
        qf = q_ref[g, pl.ds(i * tq, tq), :].astype(_F32) * (A_HEAD_DIM ** -0.5)
        qq = jnp.concatenate([jnp.where(lane < A_HEAD_DIM, qf, 0.0),
                              jnp.where(lane >= A_HEAD_DIM, qf, 0.0)], axis=0).astype(_BF16)
        return jnp.concatenate([qq, q_ones], axis=1)

    def scores(i):
        out = []
        for g in range(heads):
            s = lax.dot_general(stacked_q(g, i), kk_ref[g, pl.ds(0, (i + 1) * tq), :],
                                (((1,), (1,)), ((), ())), preferred_element_type=_F32)
            own = jnp.where(causal, s[:, i * tq:], NEG_BIG)
            out.append(own if i == 0 else jnp.concatenate([s[:, :i * tq], own], axis=1))
        return out

    def finish(i, s_blk):
        probs = []
        for g in range(heads):
            m = jnp.max(s_blk[g], axis=-1, keepdims=True)
            probs.append(jnp.exp((s_blk[g] - m).astype(_BF16)))
        for g in range(heads):
            pv = jnp.dot(probs[g], va_ref[g, pl.ds(0, (i + 1) * tq), :],
                         preferred_element_type=_F32)
            o = pv[:, :LANE] / pv[:, LANE:]
            out = o[:tq] - lam * o[tq:]
            ms = jnp.mean(out * out, axis=-1, keepdims=True)
            oa = out * lax.rsqrt(ms + RMS_EPS) * sg_ref[...] * (1.0 - lam_init)
            rows = pl.ds(i * tq, tq)
            o_ref[g, rows, :] = (oa * _silu(z_ref[g, rows, :].astype(_F32))).astype(o_ref.dtype)

    n_blocks = seq // tq
    s_next = scores(0)
    for i in range(n_blocks):
        s_cur = s_next
        if i + 1 < n_blocks:
            s_next = scores(i + 1)
        finish(i, s_cur)


def _alibi_table(slopes, seq):
    rest = slopes[:A_HEADS, None] * jnp.arange(seq, dtype=_F32)[None, :]
    pieces = []
    for _ in range(ALIBI_PIECES):
        bits = lax.bitcast_convert_type(rest, jnp.uint32) & jnp.uint32(0xFFFF0000)
        part = lax.bitcast_convert_type(bits, _F32)
        pieces.append(part.astype(_BF16))
        rest = rest - part
    tab = jnp.stack(pieces, axis=-1)
    return jnp.pad(tab, ((0, 0), (0, 0), (0, LANE - ALIBI_PIECES)))


def _attn_a(proj4, slopes, lamv, subln_g, *, lam_init, tq=256, heads=2):
    bsz, seq = proj4.shape[1], proj4.shape[2]

    def slabs(first):
        return pl.BlockSpec((heads, None, seq, LANE), lambda b, h: (first // heads + h, b, 0, 0))

    return pl.pallas_call(
        functools.partial(_attn_a_kernel, lam_init=lam_init, tq=tq, heads=heads),
        grid=(bsz, A_HEADS // heads),
        in_specs=[
            pl.BlockSpec((4, A_HEAD_DIM), lambda b, h: (0, 0)),
            slabs(SLAB_AQ), slabs(SLAB_AK), slabs(SLAB_AV), slabs(SLAB_AZ),
            pl.BlockSpec((heads, seq, LANE), lambda b, h: (h, 0, 0)),
            pl.BlockSpec((1, LANE), lambda b, h: (0, 0)),
        ],
        out_specs=slabs(0),
        out_shape=jax.ShapeDtypeStruct((SLABS, bsz, seq, LANE), _BF16),
        scratch_shapes=[pltpu.VMEM((heads, seq, 2 * LANE), _BF16)] * 2,
        compiler_params=pltpu.CompilerParams(
            dimension_semantics=("arbitrary", "arbitrary"), vmem_limit_bytes=VMEM_LIMIT),
        name="attn_a",
    )(lamv, proj4, proj4, proj4, proj4, _alibi_table(slopes, seq), subln_g.reshape(1, LANE))


def _band_bias(slope, dil):
    qi = lax.broadcasted_iota(jnp.int32, (Q_BLOCK, 2 * Q_BLOCK), 0)
    ki = lax.broadcasted_iota(jnp.int32, (Q_BLOCK, 2 * Q_BLOCK), 1)
    off = Q_BLOCK + qi - ki
    valid = (off >= 0) & (off <= Q_BLOCK)
    return jnp.where(valid, -(slope * dil) * off.astype(_F32), NEG_BIG)


BAND_LOOKAHEAD = 2


def _attn_b_kernel(slopes_ref, q1_ref, k1_ref, v1_ref, q2_ref, k2_ref, v2_ref,
                   q3_ref, k3_ref, v3_ref, z_ref, o_ref, f_ref, og_ref, lg_ref):
    seq = o_ref.shape[0]
    h = pl.program_id(1)
    group_refs = ((q1_ref, k1_ref, v1_ref), (q2_ref, k2_ref, v2_ref), (q3_ref, k3_ref, v3_ref))
    ones_blk = jnp.ones((2 * Q_BLOCK, LANE), _BF16)

    biases, blocks = [], []
    for g, (window, dil) in enumerate(B_GROUPS):
        assert window // dil == Q_BLOCK
        biases.append(_band_bias(slopes_ref[A_HEADS + g * B_HEADS + h], dil))
        qr, kr, vr = group_refs[g]
        if dil > 1:
            f_ref[g - 1, 0] = qr[...].astype(_F32)
            f_ref[g - 1, 1] = kr[...].astype(_F32)
            f_ref[g - 1, 2] = vr[...].astype(_F32)
        for r in range(dil):
            for n in range(seq // dil // Q_BLOCK):
                blocks.append((g, dil, r, n))

    def rows_of(dil, r, n):
        first = n == 0
        k0 = n * Q_BLOCK if first else (n - 1) * Q_BLOCK
        nk = Q_BLOCK if first else 2 * Q_BLOCK
        if dil == 1:
            return pl.ds(n * Q_BLOCK, Q_BLOCK), pl.ds(k0, nk), nk
        return (pl.ds(n * Q_BLOCK * dil + r, Q_BLOCK, stride=dil),
                pl.ds(k0 * dil + r, nk, stride=dil), nk)

    def load(g, part, rows):
        if g == 0:
            return group_refs[0][part][rows, :]
        return f_ref[g - 1, part, rows, :].astype(_BF16)

    def block_scores(g, dil, r, n):
        q_rows, k_rows, nk = rows_of(dil, r, n)
        s = lax.dot_general(load(g, 0, q_rows), load(g, 1, k_rows), (((1,), (1,)), ((), ())),
                            preferred_element_type=_F32)
        return s * (B_HEAD_DIM ** -0.5) + biases[g][:, 2 * Q_BLOCK - nk:]

    def block_finish(s, g, dil, r, n):
        q_rows, k_rows, nk = rows_of(dil, r, n)
        m = jnp.max(s, axis=-1, keepdims=True)
        p = jnp.exp((s - m).astype(_BF16))
        v_ones = jnp.concatenate([load(g, 2, k_rows), ones_blk[:nk]], axis=1)
        pv = jnp.dot(p, v_ones, preferred_element_type=_F32)
        l = pv[:, LANE:]
        og_ref[g, q_rows, :] = pv[:, :LANE] / l
        lg_ref[g, q_rows, :] = m + jnp.log(l)

    pending = []
    for t in range(len(blocks) + BAND_LOOKAHEAD):
        if t < len(blocks):
            pending.append(block_scores(*blocks[t]))
        if t >= BAND_LOOKAHEAD:
            block_finish(pending.pop(0), *blocks[t - BAND_LOOKAHEAD])

    rows = 256
    for c in range(seq // rows):
        sl = pl.ds(c * rows, rows)
        l0, l1, l2 = lg_ref[0, sl, :], lg_ref[1, sl, :], lg_ref[2, sl, :]
        mx = jnp.maximum(jnp.maximum(l0, l1), l2)
        w0, w1, w2 = jnp.exp(l0 - mx), jnp.exp(l1 - mx), jnp.exp(l2 - mx)
        ob = (w0 * og_ref[0, sl, :] + w1 * og_ref[1, sl, :] + w2 * og_ref[2, sl, :]) / (w0 + w1 + w2)
        o_ref[sl, :] = (ob * _silu(z_ref[sl, :].astype(_F32))).astype(o_ref.dtype)


def _attn_b(proj4, slopes):
    bsz, seq = proj4.shape[1], proj4.shape[2]

    def slab(first):
        return pl.BlockSpec((None, None, seq, LANE), lambda b, h: (first + h, b, 0, 0))

    in_specs = [pl.BlockSpec(memory_space=pltpu.SMEM)]
    for g in range(len(B_GROUPS)):
        for part in range(3):
            in_specs.append(slab(SLAB_B + (3 * g + part) * SLABS))
    in_specs.append(slab(SLAB_BZ))
    return pl.pallas_call(
        _attn_b_kernel,
        grid=(bsz, B_HEADS),
        in_specs=in_specs,
        out_specs=pl.BlockSpec((None, None, seq, LANE), lambda b, h: (h, b, 0, 0)),
        out_shape=jax.ShapeDtypeStruct((SLABS, bsz, seq, LANE), _BF16),
        scratch_shapes=[pltpu.VMEM((2, 3, seq, LANE), _F32), pltpu.VMEM((3, seq, LANE), _F32),
                        pltpu.VMEM((3, seq, LANE), _F32)],
        compiler_params=pltpu.CompilerParams(
            dimension_semantics=("arbitrary", "arbitrary"), vmem_limit_bytes=VMEM_LIMIT),
        name="attn_b",
    )(slopes, *([proj4] * 10))


CONV_HALO = 8


def _merge_kernel(ya_ref, yb_ref, cbc_ref, cxz_ref, cw_ref, gp_ref, gb_ref, x_ref, wa_ref, wb_ref,
                  wc_ref, wo_ref, ng_ref, *rest, final, tiles_per_seq):
    if final:
        o_ref, cat_ref, mrg_ref, halo_ref = rest
    else:
        o_ref, h_ref, cat_ref, mrg_ref, halo_ref = rest
    tm = x_ref.shape[0]
    branch = []
    for y_ref, w_ref in ((ya_ref, wa_ref), (yb_ref, wb_ref)):
        for c in range(SLABS):
            cat_ref[:, c * LANE:(c + 1) * LANE] = y_ref[c]
        branch.append(jnp.dot(cat_ref[...], w_ref[...], preferred_element_type=_F32))

    seq_start = pl.program_id(0) % tiles_per_seq == 0
    for c in range(SLABS):
        cols = slice(c * LANE, (c + 1) * LANE)
        u = cbc_ref[SLABS + c].astype(_F32) * cxz_ref[c].astype(_F32)
        ext = jnp.concatenate([jnp.where(seq_start, 0.0, halo_ref[c]), u], axis=0)
        halo_ref[c] = u[tm - CONV_HALO:]
        u1 = pltpu.roll(ext, 1, 0)[CONV_HALO:]
        u2 = pltpu.roll(ext, 2, 0)[CONV_HALO:]
        conv = cw_ref[0:1, cols] * u2 + cw_ref[1:2, cols] * u1 + cw_ref[2:3, cols] * u
        yc = cbc_ref[c].astype(_F32) * conv * _silu(cxz_ref[SLABS + c].astype(_F32))
        cat_ref[:, cols] = yc.astype(_BF16)
    branch.append(jnp.dot(cat_ref[...], wc_ref[...], preferred_element_type=_F32))
    per_branch = D_MODEL // LANE
    for c in range(per_branch):
        cols = slice(c * LANE, (c + 1) * LANE)
        tot = None
        for n in range(N_BRANCH):
            gcols = slice(n * D_MODEL + c * LANE, n * D_MODEL + (c + 1) * LANE)
            gate = jax.nn.sigmoid(gp_ref[n * per_branch + c].astype(_F32) + gb_ref[:, gcols])
            term = gate * branch[n][:, cols]
            tot = term if tot is None else tot + term
        mrg_ref[:, cols] = tot.astype(_BF16)
    out = x_ref[...] + jnp.dot(mrg_ref[...], wo_ref[...], preferred_element_type=_F32)
    ms = jnp.mean(out * out, axis=-1, keepdims=True)
    normed = out * lax.rsqrt(ms + RMS_EPS) * ng_ref[...]
    if final:
        o_ref[...] = normed
    else:
        o_ref[...] = out
        h_ref[...] = normed.astype(h_ref.dtype)


def _merge(ya, yb, proj, conv_w, gate_b, x2d, wa, wb, wc, wo, next_g, *, final, seq, tm=256):
    tokens = x2d.shape[0]
    assert seq % tm == 0 and SLAB_CB % (2 * SLABS) == 0 and SLAB_CX % (2 * SLABS) == 0
    row_spec = pl.BlockSpec((tm, D_MODEL), lambda i: (i, 0))
    if final:
        out_specs, out_shape = row_spec, jax.ShapeDtypeStruct((tokens, D_MODEL), _F32)
    else:
        out_specs = (row_spec, row_spec)
        out_shape = (jax.ShapeDtypeStruct((tokens, D_MODEL), _F32),
                     jax.ShapeDtypeStruct((tokens, D_MODEL), _BF16))
    const = lambda shape: pl.BlockSpec(shape, lambda i: (0,) * len(shape),
                                       pipeline_mode=pl.Buffered(1))
    yspec = pl.BlockSpec((SLABS, tm, LANE), lambda i: (0, i, 0))
    return pl.pallas_call(
        functools.partial(_merge_kernel, final=final, tiles_per_seq=seq // tm),
        grid=(tokens // tm,),
        in_specs=[
            yspec, yspec,
            pl.BlockSpec((2 * SLABS, tm, LANE), lambda i: (SLAB_CB // (2 * SLABS), i, 0)),
            pl.BlockSpec((2 * SLABS, tm, LANE), lambda i: (SLAB_CX // (2 * SLABS), i, 0)),
            const((C_CONV, WIDTH)),
            pl.BlockSpec((GATE_SLABS, tm, LANE), lambda i: (SLAB_GATE // GATE_SLABS, i, 0)),
            const((1, N_BRANCH * D_MODEL)),
            pl.BlockSpec((tm, D_MODEL), lambda i: (i, 0)),
            const((WIDTH, D_MODEL)), const((WIDTH, D_MODEL)), const((WIDTH, D_MODEL)),
            const((D_MODEL, D_MODEL)),
            const((1, D_MODEL)),
        ],
        out_specs=out_specs,
        out_shape=out_shape,
        scratch_shapes=[pltpu.VMEM((tm, WIDTH), _BF16), pltpu.VMEM((tm, D_MODEL), _BF16),
                        pltpu.VMEM((SLABS, CONV_HALO, LANE), _F32)],
        compiler_params=pltpu.CompilerParams(
            dimension_semantics=("arbitrary",), vmem_limit_bytes=VMEM_LIMIT),
        name="merge",
    )(ya, yb, proj, proj, conv_w, proj, gate_b.reshape(1, N_BRANCH * D_MODEL), x2d, wa, wb, wc, wo,
      next_g.reshape(1, D_MODEL))


def kernel(x, norm_g, w_in, gate_b, lambda_q1, lambda_k1, lambda_q2, lambda_k2, subln_g, conv_w,
           w_branch_a, w_branch_b, w_branch_c, w_out, final_g):
    bsz, seq, _ = x.shape
    assert seq == SEQ and SLAB_GATE % GATE_SLABS == 0
    tokens = bsz * seq
    slopes = jnp.asarray(2.0 ** (-8.0 * np.arange(1, N_ALIBI + 1) / N_ALIBI), dtype=_F32)
    x2d = x.reshape(tokens, D_MODEL)
    h = _rmsnorm(x2d, norm_g[0])
    for l in range(DEPTH):
        final = l == DEPTH - 1
        lam_init = 0.8 - 0.6 * math.exp(-0.3 * l)
        lamv = jnp.stack([lambda_q1[l], lambda_k1[l], lambda_q2[l], lambda_k2[l]])
        proj = _inproj(h, w_in, l)
        proj4 = proj.reshape(N_SLABS, bsz, seq, LANE)
        ya = _attn_a(proj4, slopes, lamv, subln_g[l], lam_init=lam_init)
        yb = _attn_b(proj4, slopes)
        to3 = lambda y: y.reshape(SLABS, tokens, LANE)
        res = _merge(to3(ya), to3(yb), proj, conv_w[l], gate_b[l], x2d,
                     w_branch_a[l].astype(_BF16), w_branch_b[l].astype(_BF16),
                     w_branch_c[l].astype(_BF16), w_out[l].astype(_BF16),
                     final_g if final else norm_g[l + 1], final=final, seq=seq)
        x2d, h = (res, None) if final else res
    return x2d.reshape(bsz, seq, D_MODEL)
```

```python
import functools
import math

import jax
import jax.numpy as jnp
import numpy as np
from jax import lax
from jax.experimental import pallas as pl
from jax.experimental.pallas import tpu as pltpu

D_MODEL = 2048
SEQ = 2048
DEPTH = 2
A_HEADS = 8
A_HEAD_DIM = 64
B_HEADS = 8
B_HEAD_DIM = 128
B_GROUPS = ((128, 1), (512, 4), (2048, 16))
WIDTH = 1024
C_CONV = 3
N_BRANCH = 3
Q_BLOCK = 128
N_ALIBI = A_HEADS + len(B_GROUPS) * B_HEADS
RMS_EPS = 1e-6
N_IN = 4 * WIDTH + 9 * WIDTH + WIDTH + 4 * WIDTH + N_BRANCH * D_MODEL

LANE = 128
N_SLABS = N_IN // LANE
SLABS = WIDTH // LANE
SLAB_AQ, SLAB_AK, SLAB_AV, SLAB_AZ = 0, SLABS, 2 * SLABS, 3 * SLABS
SLAB_B = 4 * SLABS
SLAB_BZ = SLAB_B + 9 * SLABS
SLAB_CB = SLAB_BZ + SLABS
SLAB_CC, SLAB_CX, SLAB_CZ = SLAB_CB + SLABS, SLAB_CB + 2 * SLABS, SLAB_CB + 3 * SLABS
SLAB_GATE = SLAB_CB + 4 * SLABS
GATE_SLABS = N_BRANCH * D_MODEL // LANE

NEG_BIG = -1e30
VMEM_LIMIT = 56 * 1024 * 1024

_BF16 = jnp.bfloat16
_F32 = jnp.float32


def _silu(z):
    return z * jax.nn.sigmoid(z)


def _rmsnorm_kernel(x_ref, g_ref, o_ref):
    x = x_ref[...]
    ms = jnp.mean(x * x, axis=-1, keepdims=True)
    o_ref[...] = (x * lax.rsqrt(ms + RMS_EPS) * g_ref[...]).astype(o_ref.dtype)


def _rmsnorm(x2d, g, *, tm=1024):
    tokens = x2d.shape[0]
    return pl.pallas_call(
        _rmsnorm_kernel,
        grid=(tokens // tm,),
        in_specs=[pl.BlockSpec((tm, D_MODEL), lambda i: (i, 0)),
                  pl.BlockSpec((1, D_MODEL), lambda i: (0, 0))],
        out_specs=pl.BlockSpec((tm, D_MODEL), lambda i: (i, 0)),
        out_shape=jax.ShapeDtypeStruct((tokens, D_MODEL), _BF16),
        compiler_params=pltpu.CompilerParams(
            dimension_semantics=("arbitrary",), vmem_limit_bytes=VMEM_LIMIT),
        name="rmsnorm",
    )(x2d, g.reshape(1, D_MODEL))


def _inproj_kernel(h_ref, w_ref, o_ref, wb_ref):
    @pl.when(pl.program_id(1) == 0)
    def _():
        wb_ref[...] = w_ref[...].astype(_BF16)

    acc = jnp.dot(h_ref[...], wb_ref[...], preferred_element_type=_F32)
    for c in range(o_ref.shape[0]):
        o_ref[c] = acc[:, c * LANE:(c + 1) * LANE].astype(o_ref.dtype)


def _inproj(h, w_in, layer, *, tm=2048, tn=1024):
    tokens = h.shape[0]
    return pl.pallas_call(
        _inproj_kernel,
        grid=(N_IN // tn, tokens // tm),
        in_specs=[
            pl.BlockSpec((tm, D_MODEL), lambda j, i: (i, 0)),
            pl.BlockSpec((None, D_MODEL, tn), lambda j, i: (layer, 0, j)),
        ],
        out_specs=pl.BlockSpec((tn // LANE, tm, LANE), lambda j, i: (j, i, 0)),
        out_shape=jax.ShapeDtypeStruct((N_SLABS, tokens, LANE), _BF16),
        scratch_shapes=[pltpu.VMEM((D_MODEL, tn), _BF16)],
        compiler_params=pltpu.CompilerParams(
            dimension_semantics=("arbitrary", "arbitrary"), vmem_limit_bytes=VMEM_LIMIT),
        name="inproj",
    )(h, w_in)


ALIBI_PIECES = 3


def _attn_a_kernel(lamv_ref, q_ref, k_ref, v_ref, z_ref, bias_ref, sg_ref, o_ref, kk_ref, va_ref,
                   *, lam_init, tq, heads):
    seq = q_ref.shape[1]
    ones_col = jnp.ones((seq, LANE), _BF16)
    for g in range(heads):
        kk_ref[g, :, :LANE] = k_ref[g]
        kk_ref[g, :, LANE:] = bias_ref[g]
        va_ref[g, :, :LANE] = v_ref[g]
        va_ref[g, :, LANE:] = ones_col

    lane = lax.broadcasted_iota(jnp.int32, (tq, LANE), 1)
    lane2 = lax.broadcasted_iota(jnp.int32, (2 * tq, LANE), 1)
    q_ones = jnp.where(lane2 < ALIBI_PIECES, 1.0, 0.0).astype(_BF16)
    row = lax.broadcasted_iota(jnp.int32, (2 * tq, tq), 0)
    causal = lax.broadcasted_iota(jnp.int32, (2 * tq, tq), 1) <= jnp.where(row >= tq, row - tq, row)
    lv = lamv_ref[...]
    lam = (jnp.exp(jnp.sum(lv[0:1] * lv[1:2], axis=-1, keepdims=True))
           - jnp.exp(jnp.sum(lv[2:3] * lv[3:4], axis=-1, keepdims=True)) + lam_init)

    def stacked_q(g, i):
        qf = q_ref[g, pl.ds(i * tq, tq), :].astype(_F32) * (A_HEAD_DIM ** -0.5)
        qq = jnp.concatenate([jnp.where(lane < A_HEAD_DIM, qf, 0.0),
                              jnp.where(lane >= A_HEAD_DIM, qf, 0.0)], axis=0).astype(_BF16)
        return jnp.concatenate([qq, q_ones], axis=1)

    def scores(i):
        out = []
        for g in range(heads):
            s = lax.dot_general(stacked_q(g, i), kk_ref[g, pl.ds(0, (i + 1) * tq), :],
                                (((1,), (1,)), ((), ())), preferred_element_type=_F32)
            own = jnp.where(causal, s[:, i * tq:], NEG_BIG)
            out.append(own if i == 0 else jnp.concatenate([s[:, :i * tq], own], axis=1))
        return out

    def finish(i, s_blk):
        probs = []
        for g in range(heads):
            m = jnp.max(s_blk[g], axis=-1, keepdims=True)
            probs.append(jnp.exp((s_blk[g] - m).astype(_BF16)))
        for g in range(heads):
            pv = jnp.dot(probs[g], va_ref[g, pl.ds(0, (i + 1) * tq), :],
                         preferred_element_type=_F32)
            o = pv[:, :LANE] / pv[:, LANE:]
            out = o[:tq] - lam * o[tq:]
            ms = jnp.mean(out * out, axis=-1, keepdims=True)
            oa = out * lax.rsqrt(ms + RMS_EPS) * sg_ref[...] * (1.0 - lam_init)
            rows = pl.ds(i * tq, tq)
            o_ref[g, rows, :] = (oa * _silu(z_ref[g, rows, :].astype(_F32))).astype(o_ref.dtype)

    n_blocks = seq // tq
    s_next = scores(0)
    for i in range(n_blocks):
        s_cur = s_next
        if i + 1 < n_blocks:
            s_next = scores(i + 1)
        finish(i, s_cur)


def _alibi_table(slopes, seq):
    rest = slopes[:A_HEADS, None] * jnp.arange(seq, dtype=_F32)[None, :]
    pieces = []
    for _ in range(ALIBI_PIECES):
        bits = lax.bitcast_convert_type(rest, jnp.uint32) & jnp.uint32(0xFFFF0000)
        part = lax.bitcast_convert_type(bits, _F32)
        pieces.append(part.astype(_BF16))
        rest = rest - part
    tab = jnp.stack(pieces, axis=-1)
    return jnp.pad(tab, ((0, 0), (0, 0), (0, LANE - ALIBI_PIECES)))


def _attn_a(proj4, slopes, lamv, subln_g, *, lam_init, tq=256, heads=2):
    bsz, seq = proj4.shape[1], proj4.shape[2]

    def slabs(first):
        return pl.BlockSpec((heads, None, seq, LANE), lambda b, h: (first // heads + h, b, 0, 0))

    return pl.pallas_call(
        functools.partial(_attn_a_kernel, lam_init=lam_init, tq=tq, heads=heads),
        grid=(bsz, A_HEADS // heads),
        in_specs=[
            pl.BlockSpec((4, A_HEAD_DIM), lambda b, h: (0, 0)),
            slabs(SLAB_AQ), slabs(SLAB_AK), slabs(SLAB_AV), slabs(SLAB_AZ),
            pl.BlockSpec((heads, seq, LANE), lambda b, h: (h, 0, 0)),
            pl.BlockSpec((1, LANE), lambda b, h: (0, 0)),
        ],
        out_specs=slabs(0),
        out_shape=jax.ShapeDtypeStruct((SLABS, bsz, seq, LANE), _BF16),
        scratch_shapes=[pltpu.VMEM((heads, seq, 2 * LANE), _BF16)] * 2,
        compiler_params=pltpu.CompilerParams(
            dimension_semantics=("arbitrary", "arbitrary"), vmem_limit_bytes=VMEM_LIMIT),
        name="attn_a",
    )(lamv, proj4, proj4, proj4, proj4, _alibi_table(slopes, seq), subln_g.reshape(1, LANE))


def _band_bias(slope, dil):
    qi = lax.broadcasted_iota(jnp.int32, (Q_BLOCK, 2 * Q_BLOCK), 0)
    ki = lax.broadcasted_iota(jnp.int32, (Q_BLOCK, 2 * Q_BLOCK), 1)
    off = Q_BLOCK + qi - ki
    valid = (off >= 0) & (off <= Q_BLOCK)
    return jnp.where(valid, -(slope * dil) * off.astype(_F32), NEG_BIG)


BAND_LOOKAHEAD = 2


def _attn_b_kernel(slopes_ref, q1_ref, k1_ref, v1_ref, q2_ref, k2_ref, v2_ref,
                   q3_ref, k3_ref, v3_ref, z_ref, o_ref, f_ref, og_ref, lg_ref):
    seq = o_ref.shape[0]
    h = pl.program_id(1)
    group_refs = ((q1_ref, k1_ref, v1_ref), (q2_ref, k2_ref, v2_ref), (q3_ref, k3_ref, v3_ref))
    ones_blk = jnp.ones((2 * Q_BLOCK, LANE), _BF16)

    biases, blocks = [], []
    for g, (window, dil) in enumerate(B_GROUPS):
        assert window // dil == Q_BLOCK
        biases.append(_band_bias(slopes_ref[A_HEADS + g * B_HEADS + h], dil))
        qr, kr, vr = group_refs[g]
        if dil > 1:
            f_ref[g - 1, 0] = qr[...].astype(_F32)
            f_ref[g - 1, 1] = kr[...].astype(_F32)
            f_ref[g - 1, 2] = vr[...].astype(_F32)
        for r in range(dil):
            for n in range(seq // dil // Q_BLOCK):
                blocks.append((g, dil, r, n))

    def rows_of(dil, r, n):
        first = n == 0
        k0 = n * Q_BLOCK if first else (n - 1) * Q_BLOCK
        nk = Q_BLOCK if first else 2 * Q_BLOCK
        if dil == 1:
            return pl.ds(n * Q_BLOCK, Q_BLOCK), pl.ds(k0, nk), nk
        return (pl.ds(n * Q_BLOCK * dil + r, Q_BLOCK, stride=dil),
                pl.ds(k0 * dil + r, nk, stride=dil), nk)

    def load(g, part, rows):
        if g == 0:
            return group_refs[0][part][rows, :]
        return f_ref[g - 1, part, rows, :].astype(_BF16)

    def block_scores(g, dil, r, n):
        q_rows, k_rows, nk = rows_of(dil, r, n)
        s = lax.dot_general(load(g, 0, q_rows), load(g, 1, k_rows), (((1,), (1,)), ((), ())),
                            preferred_element_type=_F32)
        return s * (B_HEAD_DIM ** -0.5) + biases[g][:, 2 * Q_BLOCK - nk:]

    def block_finish(s, g, dil, r, n):
        q_rows, k_rows, nk = rows_of(dil, r, n)
        m = jnp.max(s, axis=-1, keepdims=True)
        p = jnp.exp((s - m).astype(_BF16))
        v_ones = jnp.concatenate([load(g, 2, k_rows), ones_blk[:nk]], axis=1)
        pv = jnp.dot(p, v_ones, preferred_element_type=_F32)
        l = pv[:, LANE:]
        og_ref[g, q_rows, :] = pv[:, :LANE] / l
        lg_ref[g, q_rows, :] = m + jnp.log(l)

    pending = []
    for t in range(len(blocks) + BAND_LOOKAHEAD):
        if t < len(blocks):
            pending.append(block_scores(*blocks[t]))
        if t >= BAND_LOOKAHEAD:
            block_finish(pending.pop(0), *blocks[t - BAND_LOOKAHEAD])

    rows = 256
    for c in range(seq // rows):
        sl = pl.ds(c * rows, rows)
        l0, l1, l2 = lg_ref[0, sl, :], lg_ref[1, sl, :], lg_ref[2, sl, :]
        mx = jnp.maximum(jnp.maximum(l0, l1), l2)
        w0, w1, w2 = jnp.exp(l0 - mx), jnp.exp(l1 - mx), jnp.exp(l2 - mx)
        ob = (w0 * og_ref[0, sl, :] + w1 * og_ref[1, sl, :] + w2 * og_ref[2, sl, :]) / (w0 + w1 + w2)
        o_ref[sl, :] = (ob * _silu(z_ref[sl, :].astype(_F32))).astype(o_ref.dtype)


def _attn_b(proj4, slopes):
    bsz, seq = proj4.shape[1], proj4.shape[2]

    def slab(first):
        return pl.BlockSpec((None, None, seq, LANE), lambda b, h: (first + h, b, 0, 0))

    in_specs = [pl.BlockSpec(memory_space=pltpu.SMEM)]
    for g in range(len(B_GROUPS)):
        for part in range(3):
            in_specs.append(slab(SLAB_B + (3 * g + part) * SLABS))
    in_specs.append(slab(SLAB_BZ))
    return pl.pallas_call(
        _attn_b_kernel,
        grid=(bsz, B_HEADS),
        in_specs=in_specs,
        out_specs=pl.BlockSpec((None, None, seq, LANE), lambda b, h: (h, b, 0, 0)),
        out_shape=jax.ShapeDtypeStruct((SLABS, bsz, seq, LANE), _BF16),
        scratch_shapes=[pltpu.VMEM((2, 3, seq, LANE), _F32), pltpu.VMEM((3, seq, LANE), _F32),
                        pltpu.VMEM((3, seq, LANE), _F32)],
        compiler_params=pltpu.CompilerParams(
            dimension_semantics=("arbitrary", "arbitrary"), vmem_limit_bytes=VMEM_LIMIT),
        name="attn_b",
    )(slopes, *([proj4] * 10))


CONV_HALO = 8


def _merge_kernel(ya_ref, yb_ref, cbc_ref, cxz_ref, cw_ref, gp_ref, gb_ref, wa_ref, wb_ref, wc_ref,
                  mrg_ref, cat_ref, halo_ref, *, tiles_per_seq):
    tm = mrg_ref.shape[0]
    branch = []
    for y_ref, w_ref in ((ya_ref, wa_ref), (yb_ref, wb_ref)):
        for c in range(SLABS):
            cat_ref[:, c * LANE:(c + 1) * LANE] = y_ref[c]
        branch.append(jnp.dot(cat_ref[...], w_ref[...], preferred_element_type=_F32))

    seq_start = pl.program_id(0) % tiles_per_seq == 0
    for c in range(SLABS):
        cols = slice(c * LANE, (c + 1) * LANE)
        u = cbc_ref[SLABS + c].astype(_F32) * cxz_ref[c].astype(_F32)
        ext = jnp.concatenate([jnp.where(seq_start, 0.0, halo_ref[c]), u], axis=0)
        halo_ref[c] = u[tm - CONV_HALO:]
        u1 = pltpu.roll(ext, 1, 0)[CONV_HALO:]
        u2 = pltpu.roll(ext, 2, 0)[CONV_HALO:]
        conv = cw_ref[0:1, cols] * u2 + cw_ref[1:2, cols] * u1 + cw_ref[2:3, cols] * u
        yc = cbc_ref[c].astype(_F32) * conv * _silu(cxz_ref[SLABS + c].astype(_F32))
        cat_ref[:, cols] = yc.astype(_BF16)
    branch.append(jnp.dot(cat_ref[...], wc_ref[...], preferred_element_type=_F32))
    per_branch = D_MODEL // LANE
    for c in range(per_branch):
        cols = slice(c * LANE, (c + 1) * LANE)
        tot = None
        for n in range(N_BRANCH):
            gcols = slice(n * D_MODEL + c * LANE, n * D_MODEL + (c + 1) * LANE)
            gate = jax.nn.sigmoid(gp_ref[n * per_branch + c].astype(_F32) + gb_ref[:, gcols])
            term = gate * branch[n][:, cols]
            tot = term if tot is None else tot + term
        mrg_ref[:, cols] = tot.astype(_BF16)


def _outproj_kernel(m_ref, x_ref, wo_ref, ng_ref, o_ref, *h_ref, final):
    out = x_ref[...] + jnp.dot(m_ref[...], wo_ref[...], preferred_element_type=_F32)
    ms = jnp.mean(out * out, axis=-1, keepdims=True)
    normed = out * lax.rsqrt(ms + RMS_EPS) * ng_ref[...]
    if final:
        o_ref[...] = normed
    else:
        o_ref[...] = out
        h_ref[0][...] = normed.astype(h_ref[0].dtype)


def _merge(ya, yb, proj, conv_w, gate_b, x2d, wa, wb, wc, wo, next_g, *, final, seq, tm=512):
    tokens = x2d.shape[0]
    assert seq % tm == 0 and SLAB_CB % (2 * SLABS) == 0 and SLAB_CX % (2 * SLABS) == 0
    row_spec = pl.BlockSpec((tm, D_MODEL), lambda i: (i, 0))
    const = lambda shape: pl.BlockSpec(shape, lambda i: (0,) * len(shape),
                                       pipeline_mode=pl.Buffered(1))
    yspec = pl.BlockSpec((SLABS, tm, LANE), lambda i: (0, i, 0))
    params = pltpu.CompilerParams(dimension_semantics=("arbitrary",), vmem_limit_bytes=VMEM_LIMIT)
    merged = pl.pallas_call(
        functools.partial(_merge_kernel, tiles_per_seq=seq // tm),
        grid=(tokens // tm,),
        in_specs=[
            yspec, yspec,
            pl.BlockSpec((2 * SLABS, tm, LANE), lambda i: (SLAB_CB // (2 * SLABS), i, 0)),
            pl.BlockSpec((2 * SLABS, tm, LANE), lambda i: (SLAB_CX // (2 * SLABS), i, 0)),
            const((C_CONV, WIDTH)),
            pl.BlockSpec((GATE_SLABS, tm, LANE), lambda i: (SLAB_GATE // GATE_SLABS, i, 0)),
            const((1, N_BRANCH * D_MODEL)),
            const((WIDTH, D_MODEL)), const((WIDTH, D_MODEL)), const((WIDTH, D_MODEL)),
        ],
        out_specs=row_spec,
        out_shape=jax.ShapeDtypeStruct((tokens, D_MODEL), _BF16),
        scratch_shapes=[pltpu.VMEM((tm, WIDTH), _BF16), pltpu.VMEM((SLABS, CONV_HALO, LANE), _F32)],
        compiler_params=params,
        name="merge",
    )(ya, yb, proj, proj, conv_w, proj, gate_b.reshape(1, N_BRANCH * D_MODEL), wa, wb, wc)
    if final:
        out_specs, out_shape = row_spec, jax.ShapeDtypeStruct((tokens, D_MODEL), _F32)
    else:
        out_specs = (row_spec, row_spec)
        out_shape = (jax.ShapeDtypeStruct((tokens, D_MODEL), _F32),
                     jax.ShapeDtypeStruct((tokens, D_MODEL), _BF16))
    return pl.pallas_call(
        functools.partial(_outproj_kernel, final=final),
        grid=(tokens // tm,),
        in_specs=[row_spec, row_spec, const((D_MODEL, D_MODEL)), const((1, D_MODEL))],
        out_specs=out_specs,
        out_shape=out_shape,
        compiler_params=params,
        name="outproj",
    )(merged, x2d, wo, next_g.reshape(1, D_MODEL))


def kernel(x, norm_g, w_in, gate_b, lambda_q1, lambda_k1, lambda_q2, lambda_k2, subln_g, conv_w,
           w_branch_a, w_branch_b, w_branch_c, w_out, final_g):
    bsz, seq, _ = x.shape
    assert seq == SEQ and SLAB_GATE % GATE_SLABS == 0
    tokens = bsz * seq
    slopes = jnp.asarray(2.0 ** (-8.0 * np.arange(1, N_ALIBI + 1) / N_ALIBI), dtype=_F32)
    x2d = x.reshape(tokens, D_MODEL)
    h = _rmsnorm(x2d, norm_g[0])
    for l in range(DEPTH):
        final = l == DEPTH - 1
        lam_init = 0.8 - 0.6 * math.exp(-0.3 * l)
        lamv = jnp.stack([lambda_q1[l], lambda_k1[l], lambda_q2[l], lambda_k2[l]])
        proj = _inproj(h, w_in, l)
        proj4 = proj.reshape(N_SLABS, bsz, seq, LANE)
        ya = _attn_a(proj4, slopes, lamv, subln_g[l], lam_init=lam_init)
        yb = _attn_b(proj4, slopes)
        to3 = lambda y: y.reshape(SLABS, tokens, LANE)
        res = _merge(to3(ya), to3(yb), proj, conv_w[l], gate_b[l], x2d,
                     w_branch_a[l].astype(_BF16), w_branch_b[l].astype(_BF16),
                     w_branch_c[l].astype(_BF16), w_out[l].astype(_BF16),
                     final_g if final else norm_g[l + 1], final=final, seq=seq)
        x2d, h = (res, None) if final else res
    return x2d.reshape(bsz, seq, D_MODEL)
```
